```python
import jax, jax.numpy as jnp
from jax import lax
import numpy as np

D_MODEL = 1024
BATCH = 1
SEQ = 16384
DEPTH = 1
DEC_BATCH = 32
DEC_SEQ = 1
PAST_LEN = 16384
PAGE_SIZE = 128

N_MEM = 256
FOX_HEADS = 8
FOX_HEAD_DIM = 64
FOX_WIDTH = FOX_HEADS * FOX_HEAD_DIM
HG_HEADS = 4
HG_HEAD_DIM = 128
HG_WIDTH = HG_HEADS * HG_HEAD_DIM
MEM_HEADS = 4
MEM_HEAD_DIM = 128
MEM_WIDTH = MEM_HEADS * MEM_HEAD_DIM
N_BRANCH = 3
Q_BLOCK = 128
HG_CHUNK = 64
FGATE_BIAS_LO = 2.0
FGATE_BIAS_HI = 10.0
RMS_EPS = 1e-6
IN_SIZES = (FOX_WIDTH, FOX_WIDTH, FOX_WIDTH, FOX_HEADS, FOX_WIDTH,
            HG_WIDTH, HG_WIDTH, HG_WIDTH, HG_WIDTH,
            MEM_WIDTH, MEM_WIDTH,
            D_MODEL, D_MODEL, D_MODEL)
N_IN = sum(IN_SIZES)

kernel_name = 'fox_hgrn2_memxattn_gated_hybrid_step'

F32 = jnp.float32


def rmsnorm(x, g):
    xf = x.astype(F32)
    y = xf * lax.rsqrt(jnp.mean(xf * xf, axis=-1, keepdims=True) + RMS_EPS)
    return (y * g.astype(F32)).astype(x.dtype)


def in_proj(xn, w_in, b_fgate, lb):
    B, T, _ = xn.shape
    h = xn @ w_in
    idx = np.cumsum(IN_SIZES)[:-1].tolist()
    fq, fk, fv, ff, fz, hq, hf, hi, hz, mq, mz, ga, gb, gm = jnp.split(h, idx, axis=-1)
    fox_logf = jax.nn.log_sigmoid(ff.astype(F32) + b_fgate.astype(F32))
    f = lb + (1.0 - lb) * jax.nn.sigmoid(hf.astype(F32))
    hg = lambda t: t.reshape(B, T, HG_HEADS, HG_HEAD_DIM)
    fx = lambda t: t.reshape(B, T, FOX_HEADS, FOX_HEAD_DIM)
    return dict(
        fox_q=fx(fq), fox_k=fx(fk), fox_v=fx(fv), fox_logf=fox_logf, fox_z=fz,
        hg_q=hg(jax.nn.silu(hq.astype(F32))), hg_k=hg(1.0 - f), hg_logf=hg(jnp.log(f)), hg_v=hg(hi), hg_z=hz,
        mem_q=mq.reshape(B, T, MEM_HEADS, MEM_HEAD_DIM), mem_z=mz,
        ga=ga, gb=gb, gm=gm)


def fox_prompt(q, k, v, logf):
    B, T, H, Dh = q.shape
    scale = Dh ** -0.5
    c = jnp.cumsum(logf, axis=1).transpose(0, 2, 1)
    kpos = jnp.arange(T)

    def block(i):
        start = i * Q_BLOCK
        qb = lax.dynamic_slice_in_dim(q, start, Q_BLOCK, axis=1)
        cq = lax.dynamic_slice_in_dim(c, start, Q_BLOCK, axis=2)
        sc = (jnp.einsum('bqhd,bkhd->bhqk', qb, k).astype(F32) * scale
              + cq[..., None] - c[:, :, None, :])
        qpos = start + jnp.arange(Q_BLOCK)
        sc = jnp.where(kpos[None, :] <= qpos[:, None], sc, -jnp.inf)
        p = jax.nn.softmax(sc, axis=-1).astype(v.dtype)
        return jnp.einsum('bhqk,bkhd->bqhd', p, v)

    o = lax.map(block, jnp.arange(T // Q_BLOCK))
    return o.transpose(1, 0, 2, 3, 4).reshape(B, T, H, Dh)


def fox_sample(q, k_new, v_new, logf_new, k_past, v_past, logf_past):
    Dh = q.shape[-1]
    P = k_past.shape[1]
    Tn = q.shape[1]
    scale = Dh ** -0.5
    rev = lax.cumsum(logf_past.astype(F32), axis=1, reverse=True)
    r_excl = jnp.concatenate([rev[:, 1:], jnp.zeros_like(rev[:, :1])], axis=1)
    r_excl = r_excl.transpose(0, 2, 1)
    cn = jnp.cumsum(logf_new.astype(F32), axis=1).transpose(0, 2, 1)
    sp = (jnp.einsum('bqhd,bkhd->bhqk', q, k_past).astype(F32) * scale
          + cn[..., None] + r_excl[:, :, None, :])
    sn = (jnp.einsum('bqhd,bkhd->bhqk', q, k_new).astype(F32) * scale
          + cn[..., None] - cn[:, :, None, :])
    tri = jnp.tril(jnp.ones((Tn, Tn), bool))
    sn = jnp.where(tri, sn, -jnp.inf)
    p = jax.nn.softmax(jnp.concatenate([sp, sn], axis=-1), axis=-1).astype(v_new.dtype)
    return (jnp.einsum('bhqk,bkhd->bqhd', p[..., :P], v_past)
            + jnp.einsum('bhqk,bkhd->bqhd', p[..., P:], v_new))


def hgrn2(q, k, v, logf, s0, chunk):
    B, L, H, DK = q.shape
    DV = v.shape[-1]
    n = L // chunk

    def blocks(t):
        return t.astype(F32).reshape(B, n, chunk, H, t.shape[-1]).swapaxes(0, 1)

    tri = jnp.tril(jnp.ones((chunk, chunk), bool))[None, :, :, None, None]

    def step(S, inp):
        qc, kc, vc, lc = inp
        b = jnp.cumsum(lc, axis=1)
        decay = jnp.exp(jnp.where(tri, b[:, :, None] - b[:, None, :], -jnp.inf))
        att = jnp.einsum('bthk,btshk,bshk->bhts', qc, decay, kc)
        o = (jnp.einsum('bhts,bshv->bthv', att, vc)
             + jnp.einsum('bthk,bhkv->bthv', qc * jnp.exp(b), S))
        bl = b[:, -1]
        S = (jnp.exp(bl)[..., None] * S
             + jnp.einsum('bshk,bshv->bhkv', kc * jnp.exp(bl[:, None] - b), vc))
        return S, o

    S, o = lax.scan(step, s0.astype(F32), (blocks(q), blocks(k), blocks(v), blocks(logf)))
    return o.swapaxes(0, 1).reshape(B, L, H, DV), S


def mem_kv(mem, g, w_mk, w_mv):
    B = mem.shape[0]
    mn = rmsnorm(mem, g)
    mk = (mn @ w_mk).reshape(B, N_MEM, MEM_HEADS, MEM_HEAD_DIM)
    mv = (mn @ w_mv).reshape(B, N_MEM, MEM_HEADS, MEM_HEAD_DIM)
    return mk, mv


def mem_attn(q, mk, mv):
    scale = q.shape[-1] ** -0.5
    sc = jnp.einsum('bqhd,bkhd->bhqk', q, mk).astype(F32) * scale
    p = jax.nn.softmax(sc, axis=-1).astype(mv.dtype)
    return jnp.einsum('bhqk,bkhd->bqhd', p, mv)


def merge_branches(x, p, o_fox, o_hg, o_mem, hg_norm_g, w_pa, w_pb, w_pm, w_o):
    B, T, _ = x.shape
    o_hg = rmsnorm(o_hg.astype(x.dtype), hg_norm_g.reshape(HG_HEADS, HG_HEAD_DIM)).reshape(B, T, HG_WIDTH)
    ya = (o_fox.reshape(B, T, FOX_WIDTH) * jax.nn.silu(p['fox_z'])) @ w_pa
    yb = (o_hg * jax.nn.silu(p['hg_z'])) @ w_pb
    ym = (o_mem.reshape(B, T, MEM_WIDTH) * jax.nn.silu(p['mem_z'])) @ w_pm
    mixed = jax.nn.sigmoid(p['ga']) * ya + jax.nn.sigmoid(p['gb']) * yb + jax.nn.sigmoid(p['gm']) * ym
    return x + mixed @ w_o


def setup_inputs(seed: int = 0) -> dict:
    key = jax.random.key(seed)
    ks = jax.random.split(key, 24)
    n_pages = PAST_LEN // PAGE_SIZE
    n_phys = (DEC_BATCH * n_pages * 5) // 4
    nrm = lambda k, shape, s=1.0: jax.random.normal(k, shape, F32) * s
    head_bias = jnp.linspace(FGATE_BIAS_LO, FGATE_BIAS_HI, FOX_HEADS, dtype=F32)
    page_table = jax.random.permutation(ks[0], n_phys)[:DEC_BATCH * n_pages].reshape(DEC_BATCH, n_pages).astype(jnp.int32)
    return {
        'x_prompt': nrm(ks[1], (BATCH, SEQ, D_MODEL)),
        'x_sample': nrm(ks[2], (DEC_BATCH, DEC_SEQ, D_MODEL)),
        'mem_prompt': nrm(ks[3], (BATCH, N_MEM, D_MODEL)),
        'cache_fox_k': nrm(ks[4], (DEPTH, n_phys, PAGE_SIZE, FOX_HEADS, FOX_HEAD_DIM)),
        'cache_fox_v': nrm(ks[5], (DEPTH, n_phys, PAGE_SIZE, FOX_HEADS, FOX_HEAD_DIM)),
        'cache_fox_logf': jax.nn.log_sigmoid(head_bias + nrm(ks[6], (DEPTH, n_phys, PAGE_SIZE, FOX_HEADS), 0.5)),
        'page_table': page_table,
        'state_hgrn': nrm(ks[7], (DEPTH, DEC_BATCH, HG_HEADS, HG_HEAD_DIM, HG_HEAD_DIM), 0.5),
        'cache_mem_k': nrm(ks[8], (DEPTH, DEC_BATCH, N_MEM, MEM_HEADS, MEM_HEAD_DIM)),
        'cache_mem_v': nrm(ks[9], (DEPTH, DEC_BATCH, N_MEM, MEM_HEADS, MEM_HEAD_DIM)),
        'norm_g': 1.0 + nrm(ks[10], (DEPTH, D_MODEL), 0.02),
        'w_in': nrm(ks[11], (DEPTH, D_MODEL, N_IN), D_MODEL ** -0.5),
        'b_fgate': head_bias + nrm(ks[12], (DEPTH, FOX_HEADS), 0.1),
        'hg_norm_g': 1.0 + nrm(ks[13], (DEPTH, HG_WIDTH), 0.02),
        'lb_param': 1.0 + nrm(ks[14], (DEPTH + 1, HG_WIDTH), 0.1),
        'mem_norm_g': 1.0 + nrm(ks[15], (DEPTH, D_MODEL), 0.02),
        'w_mk': nrm(ks[16], (DEPTH, D_MODEL, MEM_WIDTH), D_MODEL ** -0.5),
        'w_mv': nrm(ks[17], (DEPTH, D_MODEL, MEM_WIDTH), D_MODEL ** -0.5),
        'w_pa': nrm(ks[18], (DEPTH, FOX_WIDTH, D_MODEL), FOX_WIDTH ** -0.5),
        'w_pb': nrm(ks[19], (DEPTH, HG_WIDTH, D_MODEL), HG_WIDTH ** -0.5),
        'w_pm': nrm(ks[20], (DEPTH, MEM_WIDTH, D_MODEL), MEM_WIDTH ** -0.5),
        'w_o': nrm(ks[21], (DEPTH, D_MODEL, D_MODEL), D_MODEL ** -0.5),
        'final_norm_g': 1.0 + nrm(ks[22], (D_MODEL,), 0.02),
    }


def reference(x_prompt, x_sample, mem_prompt, cache_fox_k, cache_fox_v, cache_fox_logf, page_table,
              state_hgrn, cache_mem_k, cache_mem_v, norm_g, w_in, b_fgate, hg_norm_g, lb_param,
              mem_norm_g, w_mk, w_mv, w_pa, w_pb, w_pm, w_o, final_norm_g):
    lb_all = jnp.cumsum(jax.nn.softmax(lb_param.astype(F32), axis=0), axis=0)
    xp, xs = x_prompt, x_sample
    fkp, fvp, flp, hsp, mkp, mvp = [], [], [], [], [], []
    fks, fvs, fls, hss = [], [], [], []
    for l in range(DEPTH):
        pp = in_proj(rmsnorm(xp, norm_g[l]), w_in[l], b_fgate[l], lb_all[l])
        o_fox = fox_prompt(pp['fox_q'], pp['fox_k'], pp['fox_v'], pp['fox_logf'])
        s0 = jnp.zeros((xp.shape[0], HG_HEADS, HG_HEAD_DIM, HG_HEAD_DIM), F32)
        o_hg, s_p = hgrn2(pp['hg_q'], pp['hg_k'], pp['hg_v'], pp['hg_logf'], s0, HG_CHUNK)
        mk, mv = mem_kv(mem_prompt, mem_norm_g[l], w_mk[l], w_mv[l])
        o_mem = mem_attn(pp['mem_q'], mk, mv)
        xp = merge_branches(xp, pp, o_fox, o_hg, o_mem, hg_norm_g[l], w_pa[l], w_pb[l], w_pm[l], w_o[l])
        fkp.append(pp['fox_k'])
        fvp.append(pp['fox_v'])
        flp.append(pp['fox_logf'].astype(cache_fox_logf.dtype))
        hsp.append(s_p.astype(state_hgrn.dtype))
        mkp.append(mk)
        mvp.append(mv)
        ps = in_proj(rmsnorm(xs, norm_g[l]), w_in[l], b_fgate[l], lb_all[l])
        db = xs.shape[0]
        k_past = cache_fox_k[l][page_table].reshape(db, -1, FOX_HEADS, FOX_HEAD_DIM)
        v_past = cache_fox_v[l][page_table].reshape(db, -1, FOX_HEADS, FOX_HEAD_DIM)
        lf_past = cache_fox_logf[l][page_table].reshape(db, -1, FOX_HEADS)
        o_fox_s = fox_sample(ps['fox_q'], ps['fox_k'], ps['fox_v'], ps['fox_logf'], k_past, v_past, lf_past)
        o_hg_s, s_s = hgrn2(ps['hg_q'], ps['hg_k'], ps['hg_v'], ps['hg_logf'], state_hgrn[l], xs.shape[1])
        o_mem_s = mem_attn(ps['mem_q'], cache_mem_k[l], cache_mem_v[l])
        xs = merge_branches(xs, ps, o_fox_s, o_hg_s, o_mem_s, hg_norm_g[l], w_pa[l], w_pb[l], w_pm[l], w_o[l])
        fks.append(ps['fox_k'])
        fvs.append(ps['fox_v'])
        fls.append(ps['fox_logf'].astype(cache_fox_logf.dtype))
        hss.append(s_s.astype(state_hgrn.dtype))
    y_prompt = rmsnorm(xp, final_norm_g)
    y_sample = rmsnorm(xs, final_norm_g)
    return (y_prompt, y_sample,
            jnp.stack(fkp), jnp.stack(fvp), jnp.stack(flp), jnp.stack(hsp), jnp.stack(mkp), jnp.stack(mvp),
            jnp.stack(fks), jnp.stack(fvs), jnp.stack(fls), jnp.stack(hss))
```

```python
import functools

import numpy as np
import jax
import jax.numpy as jnp
from jax import lax
from jax.experimental import pallas as pl
from jax.experimental.pallas import tpu as pltpu

F32 = jnp.float32
BF16 = jnp.bfloat16

FOX_HEADS = 8
FOX_HEAD_DIM = 64
FOX_WIDTH = FOX_HEADS * FOX_HEAD_DIM
HG_HEADS = 4
HG_HEAD_DIM = 128
HG_WIDTH = HG_HEADS * HG_HEAD_DIM
MEM_HEADS = 4
MEM_HEAD_DIM = 128
MEM_WIDTH = MEM_HEADS * MEM_HEAD_DIM
RMS_EPS = 1e-6
NEG_BIG = -1e30

LANES = 128
VMEM_LIMIT = 52 * 1024 * 1024

PROJ_TILE = 256
FOX_TILE = 512
HG_CHUNK = 64
PAGES_PER_STEP = 8


def _cparams(*sem):
    return pltpu.CompilerParams(dimension_semantics=sem, vmem_limit_bytes=VMEM_LIMIT)


def _sigmoid(x):
    return 1.0 / (1.0 + jnp.exp(-x))


def _split3(x):
    hi = x.astype(BF16)
    r1 = x - hi.astype(F32)
    mid = r1.astype(BF16)
    lo = (r1 - mid.astype(F32)).astype(BF16)
    return hi, mid, lo


def _dot(a, b):
    return jnp.dot(a, b, preferred_element_type=F32)


def _dot_nt(a, b):
    return lax.dot_general(a, b, (((1,), (1,)), ((), ())), preferred_element_type=F32)


def _dot_tn(a, b):
    return lax.dot_general(a, b, (((0,), (0,)), ((), ())), preferred_element_type=F32)


def _rms_rows(x, g):
    ms = jnp.mean(x * x, axis=-1, keepdims=True)
    return x * lax.rsqrt(ms + RMS_EPS) * g


def _lower_bound(lbp, layer):
    m = lbp[0]
    for p in lbp[1:]:
        m = jnp.maximum(m, p)
    e = [jnp.exp(p - m) for p in lbp]
    tot = e[0]
    for t in e[1:]:
        tot = tot + t
    part = e[0]
    for t in e[1:layer + 1]:
        part = part + t
    return part / tot


def _inproj_kernel(*refs, prompt, tm):
    if prompt:
        (x_ref, g_ref, wqkv_ref, wff_ref, bf_ref, wh_ref, wqT_ref, wvT_ref, l3_ref,
         k_ref, v_ref, lf_ref, kb_ref, qT_ref, vT_ref, ccol_ref, crow_ref, h3_ref, carry_ref) = refs
    else:
        (x_ref, g_ref, wqkv_ref, wff_ref, bf_ref, wh_ref, wmq_ref,
         q_ref, k_ref, v_ref, lf_ref, h3_ref, mq_ref) = refs

    xn = _rms_rows(x_ref[...], g_ref[...]).astype(BF16)
    qkv = _dot(xn, wqkv_ref[...])
    k = qkv[:, FOX_WIDTH:2 * FOX_WIDTH]
    v = qkv[:, 2 * FOX_WIDTH:]
    k_ref[...] = k
    v_ref[...] = v
    ff = _dot(xn, wff_ref[...]) + bf_ref[...]
    lf = jnp.minimum(ff, 0.0) - jnp.log1p(jnp.exp(-jnp.abs(ff)))
    lf_ref[...] = lf[:, :FOX_HEADS]
    h3_ref[...] = _dot(xn, wh_ref[...])

    if prompt:
        kb_ref[...] = k.astype(BF16)
        qT_ref[...] = _dot_nt(wqT_ref[...], xn).astype(BF16)
        vT_ref[...] = _dot_nt(wvT_ref[...], xn).astype(BF16)

        @pl.when(pl.program_id(0) == 0)
        def _():
            carry_ref[...] = jnp.zeros_like(carry_ref)

        hi, mid, lo = _split3(lf)
        c = _dot(l3_ref[...], jnp.concatenate([hi, mid, lo], axis=0)) + carry_ref[...]
        carry_ref[...] = c[tm - 1:tm, :]
        ccol_ref[...] = c[:, :FOX_HEADS]
        crow_ref[...] = jnp.transpose(c)[:FOX_HEADS, :]
    else:
        q_ref[...] = qkv[:, :FOX_WIDTH]
        mq_ref[...] = _dot(xn, wmq_ref[...])


def _inproj(x, g, w, *, prompt):
    T, D = x.shape
    tm = PROJ_TILE if prompt else T
    assert T % tm == 0
    n = T // tm
    row = lambda width: pl.BlockSpec((tm, width), lambda i: (i, 0))
    full = lambda a: pl.BlockSpec(a.shape, lambda i: (0,) * a.ndim)
    sds = jax.ShapeDtypeStruct
    if prompt:
        tri = np.tril(np.ones((tm, tm), np.float32))
        l3 = jnp.asarray(np.concatenate([tri, tri, tri], axis=1), BF16)
        ins = (x, g, w['qkv'], w['ff'], w['bf'], w['h'], w['qT'], w['vT'], l3)
        in_specs = [row(D)] + [full(a) for a in ins[1:]]
        out_shape = (sds((T, FOX_WIDTH), F32), sds((T, FOX_WIDTH), F32), sds((T, FOX_HEADS), F32),
                     sds((T, FOX_WIDTH), BF16), sds((FOX_WIDTH, T), BF16), sds((FOX_WIDTH, T), BF16),
                     sds((T, FOX_HEADS), F32), sds((FOX_HEADS, T), F32), sds((T, 3 * HG_WIDTH), F32))
        col = lambda rows: pl.BlockSpec((rows, tm), lambda i: (0, i))
        out_specs = (row(FOX_WIDTH), row(FOX_WIDTH), row(FOX_HEADS), row(FOX_WIDTH),
                     col(FOX_WIDTH), col(FOX_WIDTH), row(FOX_HEADS), col(FOX_HEADS), row(3 * HG_WIDTH))
        scratch = [pltpu.VMEM((1, LANES), F32)]
    else:
        ins = (x, g, w['qkv'], w['ff'], w['bf'], w['h'], w['mq'])
        in_specs = [row(D)] + [full(a) for a in ins[1:]]
        out_shape = (sds((T, FOX_WIDTH), F32), sds((T, FOX_WIDTH), F32), sds((T, FOX_WIDTH), F32),
                     sds((T, FOX_HEADS), F32), sds((T, 3 * HG_WIDTH), F32), sds((T, MEM_WIDTH), F32))
        out_specs = (row(FOX_WIDTH), row(FOX_WIDTH), row(FOX_WIDTH), row(FOX_HEADS),
                     row(3 * HG_WIDTH), row(MEM_WIDTH))
        scratch = []
    return pl.pallas_call(
        functools.partial(_inproj_kernel, prompt=prompt, tm=tm),
        grid=(n,), in_specs=in_specs, out_specs=out_specs, out_shape=out_shape,
        scratch_shapes=scratch, compiler_params=_cparams("arbitrary"),
        name="inproj_prompt" if prompt else "inproj_sample")(*ins)


def _fox_prompt_kernel(qi_ref, kj_ref, qT_ref, kb_ref, vT_ref, crow_ref, ccol_ref, o_ref,
                       acc_ref, m_ref, l_ref, *, tile):
    s = pl.program_id(0)
    qi = qi_ref[s]
    kj = kj_ref[s]
    hd = FOX_HEAD_DIM

    @pl.when(kj == 0)
    def _():
        acc_ref[...] = jnp.zeros_like(acc_ref)
        m_ref[...] = jnp.full_like(m_ref, NEG_BIG)
        l_ref[...] = jnp.zeros_like(l_ref)

    def step(diag):
        if diag:
            causal = (lax.broadcasted_iota(jnp.int32, (tile, tile), 0)
                      <= lax.broadcasted_iota(jnp.int32, (tile, tile), 1))
        zeros = jnp.zeros((hd, tile), BF16)
        for h in range(FOX_HEADS):
            pair, half = divmod(h, 2)
            qh = qT_ref[h * hd:(h + 1) * hd, :]
            qa = jnp.concatenate([qh, zeros] if half == 0 else [zeros, qh], axis=0)
            sT = _dot(kb_ref[:, pair * LANES:(pair + 1) * LANES], qa)
            sT = sT + crow_ref[h:h + 1, :] - ccol_ref[:, h:h + 1]
            if diag:
                sT = jnp.where(causal, sT, NEG_BIG)
            m_old = m_ref[h:h + 1, :]
            m_new = jnp.maximum(m_old, jnp.max(sT, axis=0, keepdims=True))
            alpha = jnp.exp(m_old - m_new)
            pT = jnp.exp(sT - m_new)
            l_ref[h:h + 1, :] = alpha * l_ref[h:h + 1, :] + jnp.sum(pT, axis=0, keepdims=True)
            m_ref[h:h + 1, :] = m_new
            pv = _dot(vT_ref[h * hd:(h + 1) * hd, :], pT.astype(BF16))
            acc_ref[h * hd:(h + 1) * hd, :] = alpha * acc_ref[h * hd:(h + 1) * hd, :] + pv

    @pl.when(kj < qi)
    def _():
        step(False)

    @pl.when(kj == qi)
    def _():
        step(True)
        inv = 1.0 / l_ref[...]
        for h in range(FOX_HEADS):
            acc_ref[h * hd:(h + 1) * hd, :] = acc_ref[h * hd:(h + 1) * hd, :] * inv[h:h + 1, :]
        o_ref[...] = jnp.transpose(acc_ref[...])


def _fox_prompt(qT, kb, vT, crow, ccol):
    T = kb.shape[0]
    tile = min(FOX_TILE, T)
    assert T % tile == 0
    n = T // tile
    pairs = [(i, j) for i in range(n) for j in range(i + 1)]
    qi = jnp.asarray([p[0] for p in pairs], jnp.int32)
    kj = jnp.asarray([p[1] for p in pairs], jnp.int32)
    grid_spec = pltpu.PrefetchScalarGridSpec(
        num_scalar_prefetch=2, grid=(len(pairs),),
        in_specs=[
            pl.BlockSpec((FOX_WIDTH, tile), lambda s, qi, kj: (0, qi[s])),
            pl.BlockSpec((tile, FOX_WIDTH), lambda s, qi, kj: (kj[s], 0)),
            pl.BlockSpec((FOX_WIDTH, tile), lambda s, qi, kj: (0, kj[s])),
            pl.BlockSpec((FOX_HEADS, tile), lambda s, qi, kj: (0, qi[s])),
            pl.BlockSpec((tile, FOX_HEADS), lambda s, qi, kj: (kj[s], 0)),
        ],
        out_specs=pl.BlockSpec((tile, FOX_WIDTH), lambda s, qi, kj: (qi[s], 0)),
        scratch_shapes=[pltpu.VMEM((FOX_WIDTH, tile), F32), pltpu.VMEM((FOX_HEADS, tile), F32),
                        pltpu.VMEM((FOX_HEADS, tile), F32)])
    return pl.pallas_call(
        functools.partial(_fox_prompt_kernel, tile=tile),
        grid_spec=grid_spec, out_shape=jax.ShapeDtypeStruct((T, FOX_WIDTH), F32),
        compiler_params=_cparams("arbitrary"), name="fox_prompt")(qi, kj, qT, kb, vT, crow, ccol)


def _hgrn_tables(C):
    levels = []
    B = C
    while B >= 2:
        levels.append(B)
        B //= 2
    nl = len(levels)
    t = np.arange(C)
    coef = np.zeros((nl + 2, C, C), np.float32)
    mask = np.zeros((nl + 1, C, C), np.float32)
    for li, B in enumerate(levels):
        h = B // 2
        pos = t % B
        ref = t - pos + h - 1
        for i in range(C):
            if pos[i] >= h:
                coef[li, i, ref[i] + 1:i + 1] = 1.0
            else:
                coef[li, i, i + 1:ref[i] + 1] = 1.0
        same = (t[:, None] // B) == (t[None, :] // B)
        mask[li] = same & (pos[:, None] >= h) & (pos[None, :] < h)
    mask[nl] = np.eye(C)
    coef[nl] = np.tril(np.ones((C, C)))
    coef[nl + 1] = np.triu(np.ones((C, C)), 1)
    coef = coef.reshape((nl + 2) * C, C)
    coef3 = np.concatenate([coef, coef, coef], axis=1)
    return nl, jnp.asarray(coef3, BF16), jnp.asarray(mask, F32)


def _hgrn_prompt_kernel(h3_ref, lbp_ref, g_ref, coef_ref, mask_ref, o_ref, s_out_ref, st_ref,
                        *, C, nl, layer, n_slots):
    i = pl.program_id(0)
    dk = HG_HEAD_DIM

    @pl.when(i == 0)
    def _():
        st_ref[...] = jnp.zeros_like(st_ref)

    lb = _lower_bound([lbp_ref[j:j + 1, :] for j in range(n_slots)], layer)
    hq = h3_ref[:, :HG_WIDTH]
    hf = h3_ref[:, HG_WIDTH:2 * HG_WIDTH]
    v_all = h3_ref[:, 2 * HG_WIDTH:]
    q_all = hq * _sigmoid(hq)
    f = lb + (1.0 - lb) * _sigmoid(hf)
    k_all = 1.0 - f
    hi, mid, lo = _split3(jnp.log(f))
    d_all = _dot(coef_ref[...], jnp.concatenate([hi, mid, lo], axis=0))

    for h in range(HG_HEADS):
        cs = slice(h * dk, (h + 1) * dk)
        q, k, v = q_all[:, cs], k_all[:, cs], v_all[:, cs]
        vb = v.astype(BF16)
        att = mask_ref[nl] * _dot_nt(q.astype(BF16), k.astype(BF16))
        for li in range(nl):
            e = jnp.exp(d_all[li * C:(li + 1) * C, cs])
            att = att + mask_ref[li] * _dot_nt((q * e).astype(BF16), (k * e).astype(BF16))
        st = st_ref[h]
        eb = jnp.exp(d_all[nl * C:(nl + 1) * C, cs])
        o = _dot(att.astype(BF16), vb) + _dot_nt((q * eb).astype(BF16), st.astype(BF16))
        ek = jnp.exp(d_all[(nl + 1) * C:(nl + 2) * C, cs])
        st_new = st * eb[C - 1:C, :] + _dot_tn(vb, (k * ek).astype(BF16))
        st_ref[h] = st_new
        o_ref[:, cs] = _rms_rows(o, g_ref[:, cs])

    @pl.when(i == pl.num_programs(0) - 1)
    def _():
        for h in range(HG_HEADS):
            s_out_ref[h] = jnp.transpose(st_ref[h])


def _hgrn_prompt(h3, lb_param, hg_g, layer):
    T = h3.shape[0]
    C = min(HG_CHUNK, T)
    assert T % C == 0
    nl, coef3, mask = _hgrn_tables(C)
    n_slots = lb_param.shape[0]
    full = lambda a: pl.BlockSpec(a.shape, lambda i: (0,) * a.ndim)
    return pl.pallas_call(
        functools.partial(_hgrn_prompt_kernel, C=C, nl=nl, layer=layer, n_slots=n_slots),
        grid=(T // C,),
        in_specs=[pl.BlockSpec((C, 3 * HG_WIDTH), lambda i: (i, 0)), full(lb_param), full(hg_g),
                  full(coef3), full(mask)],
        out_specs=(pl.BlockSpec((C, HG_WIDTH), lambda i: (i, 0)),
                   pl.BlockSpec((HG_HEADS, HG_HEAD_DIM, HG_HEAD_DIM), lambda i: (0, 0, 0))),
        out_shape=(jax.ShapeDtypeStruct((T, HG_WIDTH), F32),
                   jax.ShapeDtypeStruct((HG_HEADS, HG_HEAD_DIM, HG_HEAD_DIM), F32)),
        scratch_shapes=[pltpu.VMEM((HG_HEADS, HG_HEAD_DIM, HG_HEAD_DIM), F32)],
        compiler_params=_cparams("arbitrary"), name="hgrn_prompt")(h3, lb_param, hg_g, coef3, mask)


def _mem_kv_kernel(mem_ref, g_ref, wk_ref, wv_ref, mk_ref, mv_ref):
    mn = _rms_rows(mem_ref[...], g_ref[...]).astype(BF16)
    mk_ref[...] = _dot(mn, wk_ref[...])
    mv_ref[...] = _dot(mn, wv_ref[...])


def _mem_kv(mem, g, wk, wv):
    n = mem.shape[0]
    sds = jax.ShapeDtypeStruct((n, MEM_WIDTH), F32)
    return pl.pallas_call(_mem_kv_kernel, out_shape=(sds, sds),
                          compiler_params=pltpu.CompilerParams(vmem_limit_bytes=VMEM_LIMIT),
                          name="mem_kv")(mem, g, wk, wv)


def _merge_kernel(*refs, shared_mem, final):
    if shared_mem:
        (x_ref, g_ref, ofox_ref, ohg_ref, wmq_ref, mk_ref, mv_ref,
         wz_ref, wg_ref, wpa_ref, wpb_ref, wpm_ref, wo_ref, fg_ref, y_ref) = refs
    else:
        (x_ref, g_ref, ofox_ref, ohg_ref, omem_ref,
         wz_ref, wg_ref, wpa_ref, wpb_ref, wpm_ref, wo_ref, fg_ref, y_ref) = refs
    x = x_ref[...]
    xn = _rms_rows(x, g_ref[...]).astype(BF16)

    if shared_mem:
        mq = _dot(xn, wmq_ref[...])
        scale = MEM_HEAD_DIM ** -0.5
        heads = []
        for h in range(MEM_HEADS):
            cs = slice(h * MEM_HEAD_DIM, (h + 1) * MEM_HEAD_DIM)
            sc = _dot_nt(mq[:, cs].astype(BF16), mk_ref[:, cs]) * scale
            p = jnp.exp(sc - jnp.max(sc, axis=-1, keepdims=True))
            p = p / jnp.sum(p, axis=-1, keepdims=True)
            heads.append(_dot(p.astype(BF16), mv_ref[:, cs]))
        o_mem = jnp.concatenate(heads, axis=-1)
    else:
        o_mem = omem_ref[...]

    z = _dot(xn, wz_ref[...])
    sz = z * _sigmoid(z)
    ya = _dot((ofox_ref[...] * sz[:, :FOX_WIDTH]).astype(BF16), wpa_ref[...])
    yb = _dot((ohg_ref[...] * sz[:, FOX_WIDTH:FOX_WIDTH + HG_WIDTH]).astype(BF16), wpb_ref[...])
    ym = _dot((o_mem * sz[:, FOX_WIDTH + HG_WIDTH:]).astype(BF16), wpm_ref[...])
    d = x.shape[1]
    gates = _sigmoid(_dot(xn, wg_ref[...]))
    mixed = gates[:, :d] * ya + gates[:, d:2 * d] * yb + gates[:, 2 * d:] * ym
    out = x + _dot(mixed.astype(BF16), wo_ref[...])
    y_ref[...] = _rms_rows(out, fg_ref[...]) if final else out


def _merge(x, g, o_fox, o_hg, w, final_g, *, final, mem_kv=None, o_mem=None):
    T, D = x.shape
    shared = mem_kv is not None
    tm = min(PROJ_TILE, T)
    assert T % tm == 0
    row = lambda width: pl.BlockSpec((tm, width), lambda i: (i, 0))
    full = lambda a: pl.BlockSpec(a.shape, lambda i: (0,) * a.ndim)
    tail = (w['z'], w['g'], w['pa'], w['pb'], w['pm'], w['o'], final_g)
    if shared:
        ins = (x, g, o_fox, o_hg, w['mq'], mem_kv[0], mem_kv[1]) + tail
        in_specs = [row(D), full(g), row(FOX_WIDTH), row(HG_WIDTH)] + [full(a) for a in ins[4:]]
    else:
        ins = (x, g, o_fox, o_hg, o_mem) + tail
        in_specs = ([row(D), full(g), row(FOX_WIDTH), row(HG_WIDTH), row(MEM_WIDTH)]
                    + [full(a) for a in ins[5:]])
    return pl.pallas_call(
        functools.partial(_merge_kernel, shared_mem=shared, final=final),
        grid=(T // tm,), in_specs=in_specs, out_specs=row(D),
        out_shape=jax.ShapeDtypeStruct((T, D), F32),
        compiler_params=_cparams("arbitrary"),
        name="merge_prompt" if shared else "merge_sample")(*ins)


def _masked_scores(q_bf, kx_bf, n_heads, bias=None):
    s = _dot_nt(q_bf, kx_bf)
    if bias is not None:
        s = s + bias
    own = (lax.broadcasted_iota(jnp.int32, s.shape, 1) % n_heads
           == lax.broadcasted_iota(jnp.int32, s.shape, 0))
    return jnp.where(own, s, NEG_BIG)


def _online_update(s, vx_bf, m_ref, l_ref, acc_ref):
    m_old = m_ref[...]
    m_new = jnp.maximum(m_old, jnp.max(s, axis=-1, keepdims=True))
    alpha = jnp.exp(m_old - m_new)
    p = jnp.exp(s - m_new)
    l_ref[...] = alpha * l_ref[...] + jnp.sum(p, axis=-1, keepdims=True)
    m_ref[...] = m_new
    acc_ref[...] = alpha * acc_ref[...] + _dot(p.astype(BF16), vx_bf)


def _fox_decode_kernel(*refs, G, page):
    pt_ref = refs[0]
    k_refs = refs[1:1 + G]
    v_refs = refs[1 + G:1 + 2 * G]
    lf_refs = refs[1 + 2 * G:1 + 3 * G]
    q_ref, kn_ref, vn_ref, cn_ref, o_ref, m_ref, l_ref, acc_ref, carry_ref = refs[1 + 3 * G:]
    del pt_ref
    step = pl.program_id(1)
    nh, hd = FOX_HEADS, FOX_HEAD_DIM
    width = page * nh

    @pl.when(step == 0)
    def _():
        m_ref[...] = jnp.full_like(m_ref, NEG_BIG)
        l_ref[...] = jnp.zeros_like(l_ref)
        acc_ref[...] = jnp.zeros_like(acc_ref)
        carry_ref[...] = jnp.zeros_like(carry_ref)

    q = q_ref[...]
    q_bf = q.astype(BF16)

    lf = jnp.concatenate([r[...] for r in lf_refs], axis=0)
    lane = lax.broadcasted_iota(jnp.int32, lf.shape, 1)
    suf = lf
    tot = lf
    sh = nh
    while sh < width:
        suf = suf + jnp.where(lane + sh < width, pltpu.roll(suf, width - sh, 1), 0.0)
        tot = tot + pltpu.roll(tot, sh, 1)
        sh *= 2
    later = carry_ref[...]
    for g in range(G):
        bias = (suf[g:g + 1, :] - lf[g:g + 1, :]) + later + cn_ref[...]
        later = later + tot[g:g + 1, :]
        kx = k_refs[g][...].reshape(width, hd).astype(BF16)
        vx = v_refs[g][...].reshape(width, hd).astype(BF16)
        s = _masked_scores(q_bf, kx, nh, bias)
        _online_update(s, vx, m_ref, l_ref, acc_ref)
    carry_ref[...] = later

    @pl.when(step == pl.num_programs(1) - 1)
    def _():
        s_new = jnp.sum(q * kn_ref[...], axis=-1, keepdims=True)
        m_old = m_ref[...]
        m_new = jnp.maximum(m_old, s_new)
        alpha = jnp.exp(m_old - m_new)
        p_new = jnp.exp(s_new - m_new)
        l = alpha * l_ref[...] + p_new
        o_ref[...] = (alpha * acc_ref[...] + p_new * vn_ref[...]) / l


def _fox_decode(page_table, cache_k, cache_v, cache_lf_flat, q, k_new, v_new, cn_tiled, layer):
    DB, n_pages = page_table.shape
    page = cache_k.shape[2]
    G = min(PAGES_PER_STEP, n_pages)
    assert n_pages % G == 0
    nh, hd = FOX_HEADS, FOX_HEAD_DIM
    width = page * nh

    def page_map(g, tail):
        def index_map(b, s, pt):
            return (layer, pt[b, n_pages - 1 - (s * G + g)]) + tail
        return index_map

    kv_specs = [pl.BlockSpec((None, None, page, nh, hd), page_map(g, (0, 0, 0))) for g in range(G)]
    lf_specs = [pl.BlockSpec((None, None, 1, width), page_map(g, (0, 0))) for g in range(G)]
    per_b = lambda shape: pl.BlockSpec((None,) + shape, lambda b, s, pt: (b,) + (0,) * len(shape))
    grid_spec = pltpu.PrefetchScalarGridSpec(
        num_scalar_prefetch=1, grid=(DB, n_pages // G),
        in_specs=kv_specs + kv_specs + lf_specs + [per_b((nh, hd)), per_b((nh, hd)), per_b((nh, hd)),
                                                   per_b((1, width))],
        out_specs=per_b((nh, hd)),
        scratch_shapes=[pltpu.VMEM((nh, 1), F32), pltpu.VMEM((nh, 1), F32), pltpu.VMEM((nh, hd), F32),
                        pltpu.VMEM((1, width), F32)])
    return pl.pallas_call(
        functools.partial(_fox_decode_kernel, G=G, page=page),
        grid_spec=grid_spec, out_shape=jax.ShapeDtypeStruct((DB, nh, hd), F32),
        compiler_params=_cparams("arbitrary", "arbitrary"), name="fox_decode")(
            page_table, *([cache_k] * G), *([cache_v] * G), *([cache_lf_flat] * G),
            q, k_new, v_new, cn_tiled)


def _sample_mix_kernel(st_ref, hq_ref, hf_ref, hi_ref, lbp_ref, g_ref, mq_ref, mk_ref, mv_ref,
                       st_out_ref, ohg_ref, omem_ref, *, layer, n_slots):
    for h in range(HG_HEADS):
        lb = _lower_bound([lbp_ref[j, h] for j in range(n_slots)], layer)
        hq = hq_ref[h]
        q = hq * _sigmoid(hq)
        f = lb + (1.0 - lb) * _sigmoid(hf_ref[h])
        s_new = f * st_ref[h] + (1.0 - f) * hi_ref[h]
        st_out_ref[h] = s_new
        o = jnp.sum(s_new * q, axis=0, keepdims=True)
        ohg_ref[h] = _rms_rows(o, g_ref[h])

    q = mq_ref[...] * (MEM_HEAD_DIM ** -0.5)
    q8 = jnp.concatenate([q, jnp.zeros_like(q)], axis=0).astype(BF16)
    s = _masked_scores(q8, mk_ref[...].astype(BF16), MEM_HEADS)
    p = jnp.exp(s - jnp.max(s, axis=-1, keepdims=True))
    o = _dot(p.astype(BF16), mv_ref[...].astype(BF16)) / jnp.sum(p, axis=-1, keepdims=True)
    omem_ref[...] = o[:MEM_HEADS, :]


def _sample_mix(state, hq_col, hf_col, hi_row, lbp_col, hg_g_row, mq, mk_flat, mv_flat, layer):
    DB = state.shape[0]
    n_slots = lbp_col.shape[0]
    per_b = lambda a: pl.BlockSpec((None,) + a.shape[1:], lambda b: (b,) + (0,) * (a.ndim - 1))
    full = lambda a: pl.BlockSpec(a.shape, lambda b: (0,) * a.ndim)
    ins = (state, hq_col, hf_col, hi_row, lbp_col, hg_g_row, mq, mk_flat, mv_flat)
    in_specs = [per_b(state), per_b(hq_col), per_b(hf_col), per_b(hi_row), full(lbp_col), full(hg_g_row),
                per_b(mq), per_b(mk_flat), per_b(mv_flat)]
    out_shape = (jax.ShapeDtypeStruct(state.shape, F32),
                 jax.ShapeDtypeStruct((DB, HG_HEADS, 1, HG_HEAD_DIM), F32),
                 jax.ShapeDtypeStruct((DB, MEM_HEADS, MEM_HEAD_DIM), F32))
    out_specs = tuple(pl.BlockSpec((None,) + s.shape[1:], lambda b, n=len(s.shape): (b,) + (0,) * (n - 1))
                      for s in out_shape)
    return pl.pallas_call(
        functools.partial(_sample_mix_kernel, layer=layer, n_slots=n_slots),
        grid=(DB,), in_specs=in_specs, out_specs=out_specs, out_shape=out_shape,
        compiler_params=_cparams("arbitrary"), name="sample_mix")(*ins)


def _layer_weights(w_in, b_fgate, w_mk, w_mv, w_pa, w_pb, w_pm, w_o):
    fw, hw, mw = FOX_WIDTH, HG_WIDTH, MEM_WIDTH
    d = w_in.shape[0]
    o = 0
    qkv = w_in[:, o:o + 3 * fw]; o += 3 * fw
    ff = w_in[:, o:o + FOX_HEADS]; o += FOX_HEADS
    fz = w_in[:, o:o + fw]; o += fw
    h3 = w_in[:, o:o + 3 * hw]; o += 3 * hw
    hz = w_in[:, o:o + hw]; o += hw
    mq = w_in[:, o:o + mw]; o += mw
    mz = w_in[:, o:o + mw]; o += mw
    gates = w_in[:, o:o + 3 * d]; o += 3 * d
    assert o == w_in.shape[1]
    scale = FOX_HEAD_DIM ** -0.5
    qkv = jnp.concatenate([qkv[:, :fw] * scale, qkv[:, fw:]], axis=1).astype(BF16)
    return dict(
        qkv=qkv, qT=qkv[:, :fw].T, vT=qkv[:, 2 * fw:].T,
        ff=jnp.pad(ff, ((0, 0), (0, LANES - FOX_HEADS))).astype(BF16),
        bf=jnp.pad(b_fgate.astype(F32), (0, LANES - FOX_HEADS)).reshape(1, LANES),
        h=h3.astype(BF16), mq=mq.astype(BF16),
        z=jnp.concatenate([fz, hz, mz], axis=1).astype(BF16), g=gates.astype(BF16),
        mk=w_mk.astype(BF16), mv=w_mv.astype(BF16),
        pa=w_pa.astype(BF16), pb=w_pb.astype(BF16), pm=w_pm.astype(BF16), o=w_o.astype(BF16))


def kernel(x_prompt, x_sample, mem_prompt, cache_fox_k, cache_fox_v, cache_fox_logf, page_table, state_hgrn, cache_mem_k, cache_mem_v, norm_g, w_in, b_fgate, hg_norm_g, lb_param, mem_norm_g, w_mk, w_mv, w_pa, w_pb, w_pm, w_o, final_norm_g):
    depth = w_in.shape[0]
    B, T, D = x_prompt.shape
    DB, Tn, _ = x_sample.shape
    assert B == 1 and Tn == 1
    n_phys, page = cache_fox_k.shape[1], cache_fox_k.shape[2]
    n_mem = cache_mem_k.shape[2]
    n_slots = lb_param.shape[0]

    xp = x_prompt.reshape(T, D)
    xs = x_sample.reshape(DB, D)
    mem = mem_prompt.reshape(n_mem, D)
    lf_flat = cache_fox_logf.reshape(depth, n_phys, 1, page * FOX_HEADS)
    mk_cache = cache_mem_k.reshape(depth, DB, n_mem * MEM_HEADS, MEM_HEAD_DIM)
    mv_cache = cache_mem_v.reshape(depth, DB, n_mem * MEM_HEADS, MEM_HEAD_DIM)
    lbp_col = lb_param.reshape(n_slots, HG_HEADS, HG_HEAD_DIM, 1)
    final_g = final_norm_g.reshape(1, D)
    row = lambda a: a.reshape(1, -1)

    outs = [[] for _ in range(10)]
    for l in range(depth):
        w = _layer_weights(w_in[l], b_fgate[l], w_mk[l], w_mv[l], w_pa[l], w_pb[l], w_pm[l], w_o[l])
        g = row(norm_g[l])
        last = l == depth - 1

        k, v, lf, kb, qT, vT, ccol, crow, h3 = _inproj(xp, g, w, prompt=True)
        o_fox = _fox_prompt(qT, kb, vT, crow, ccol)
        o_hg, s_p = _hgrn_prompt(h3, lb_param, row(hg_norm_g[l]), l)
        mk, mv = _mem_kv(mem, row(mem_norm_g[l]), w['mk'], w['mv'])
        xp = _merge(xp, g, o_fox, o_hg, w, final_g, final=last,
                    mem_kv=(mk.astype(BF16), mv.astype(BF16)))

        qs, ks, vs, lfs, h3s, mqs = _inproj(xs, g, w, prompt=False)
        heads = lambda a: a.reshape(DB, FOX_HEADS, FOX_HEAD_DIM)
        o_fox_s = _fox_decode(page_table, cache_fox_k, cache_fox_v, lf_flat, heads(qs), heads(ks), heads(vs),
                              jnp.tile(lfs, (1, page)).reshape(DB, 1, page * FOX_HEADS), l)
        col = lambda a: a.reshape(DB, HG_HEADS, HG_HEAD_DIM, 1)
        s_s, o_hg_s, o_mem_s = _sample_mix(
            state_hgrn[l], col(h3s[:, :HG_WIDTH]), col(h3s[:, HG_WIDTH:2 * HG_WIDTH]),
            h3s[:, 2 * HG_WIDTH:].reshape(DB, HG_HEADS, 1, HG_HEAD_DIM), lbp_col,
            hg_norm_g[l].reshape(HG_HEADS, 1, HG_HEAD_DIM),
            mqs.reshape(DB, MEM_HEADS, MEM_HEAD_DIM), mk_cache[l], mv_cache[l], l)
        xs = _merge(xs, g, o_fox_s.reshape(DB, FOX_WIDTH), o_hg_s.reshape(DB, HG_WIDTH), w, final_g,
                    final=last, o_mem=o_mem_s.reshape(DB, MEM_WIDTH))

        new = (k.reshape(B, T, FOX_HEADS, FOX_HEAD_DIM), v.reshape(B, T, FOX_HEADS, FOX_HEAD_DIM),
               lf.reshape(B, T, FOX_HEADS), s_p.reshape(B, HG_HEADS, HG_HEAD_DIM, HG_HEAD_DIM),
               mk.reshape(B, n_mem, MEM_HEADS, MEM_HEAD_DIM), mv.reshape(B, n_mem, MEM_HEADS, MEM_HEAD_DIM),
               ks.reshape(DB, Tn, FOX_HEADS, FOX_HEAD_DIM), vs.reshape(DB, Tn, FOX_HEADS, FOX_HEAD_DIM),
               lfs.reshape(DB, Tn, FOX_HEADS), s_s)
        for acc, a in zip(outs, new):
            acc.append(a)

    return (xp.reshape(B, T, D), xs.reshape(DB, Tn, D)) + tuple(jnp.stack(a) for a in outs)
```

```python
import functools

import numpy as np
import jax
import jax.numpy as jnp
from jax import lax
from jax.experimental import pallas as pl
from jax.experimental.pallas import tpu as pltpu

F32 = jnp.float32
BF16 = jnp.bfloat16

FOX_HEADS = 8
FOX_HEAD_DIM = 64
FOX_WIDTH = FOX_HEADS * FOX_HEAD_DIM
HG_HEADS = 4
HG_HEAD_DIM = 128
HG_WIDTH = HG_HEADS * HG_HEAD_DIM
MEM_HEADS = 4
MEM_HEAD_DIM = 128
MEM_WIDTH = MEM_HEADS * MEM_HEAD_DIM
RMS_EPS = 1e-6
NEG_BIG = -1e30
LOG2E = 1.4426950408889634

LANES = 128
VMEM_LIMIT = 52 * 1024 * 1024

PROJ_TILE = 256
FOX_TILE = 512
FOX_LOOKAHEAD = 3
HG_CHUNK = 64
PAGES_PER_STEP = 16

AUG = 2 * FOX_HEAD_DIM
AUG_WIDTH = FOX_HEADS * AUG
N_SPLIT = 3


def _cparams(*sem):
    return pltpu.CompilerParams(dimension_semantics=sem, vmem_limit_bytes=VMEM_LIMIT)


def _sigmoid(x):
    return 1.0 / (1.0 + jnp.exp(-x))


def _split3(x):
    hi = x.astype(BF16)
    r1 = x - hi.astype(F32)
    mid = r1.astype(BF16)
    lo = (r1 - mid.astype(F32)).astype(BF16)
    return hi, mid, lo


def _dot(a, b):
    return jnp.dot(a, b, preferred_element_type=F32)


def _dot_nt(a, b):
    return lax.dot_general(a, b, (((1,), (1,)), ((), ())), preferred_element_type=F32)


def _dot_tn(a, b):
    return lax.dot_general(a, b, (((0,), (0,)), ((), ())), preferred_element_type=F32)


def _rms_rows(x, g):
    ms = jnp.mean(x * x, axis=-1, keepdims=True)
    return x * lax.rsqrt(ms + RMS_EPS) * g


def _lower_bound(lbp, layer):
    m = lbp[0]
    for p in lbp[1:]:
        m = jnp.maximum(m, p)
    e = [jnp.exp(p - m) for p in lbp]
    tot = e[0]
    for t in e[1:]:
        tot = tot + t
    part = e[0]
    for t in e[1:layer + 1]:
        part = part + t
    return part / tot


def _aug_tables():
    pk = np.zeros((N_SPLIT * LANES, AUG_WIDTH), np.float32)
    pq = np.zeros((AUG_WIDTH, N_SPLIT * LANES), np.float32)
    kc = np.zeros((1, AUG_WIDTH), np.float32)
    qc = np.zeros((AUG_WIDTH, 1), np.float32)
    for h in range(FOX_HEADS):
        for s in range(N_SPLIT):
            pk[s * LANES + h, h * AUG + FOX_HEAD_DIM + s] = -1.0
            qc[h * AUG + FOX_HEAD_DIM + s, 0] = 1.0
            kc[0, h * AUG + FOX_HEAD_DIM + N_SPLIT + s] = 1.0
            pq[h * AUG + FOX_HEAD_DIM + N_SPLIT + s, s * LANES + h] = 1.0
    return jnp.asarray(pk, BF16), jnp.asarray(pq, BF16), jnp.asarray(kc), jnp.asarray(qc)


def _log_sigmoid(x):
    return jnp.minimum(x, 0.0) - jnp.log1p(jnp.exp(-jnp.abs(x)))


def _inproj_prompt_kernel(x_ref, g_ref, wkT_ref, wvT_ref, wkaug_ref, wqaugT_ref, wff_ref, bf_ref, wh_ref,
                          l3_ref, pk_ref, pq_ref, kc_ref, qc_ref,
                          kT_ref, vT_ref, vTb_ref, kaug_ref, qaug_ref, lfT_ref, h3_ref, carry_ref, *, tm):
    xn = _rms_rows(x_ref[...], g_ref[...]).astype(BF16)
    kT_ref[...] = _dot_nt(wkT_ref[...], xn)
    vT = _dot_nt(wvT_ref[...], xn)
    vT_ref[...] = vT
    vTb_ref[...] = vT.astype(BF16)
    lf = _log_sigmoid(_dot(xn, wff_ref[...]) + bf_ref[...])
    lfT_ref[...] = jnp.transpose(lf)[:FOX_HEADS, :]
    h3_ref[...] = _dot(xn, wh_ref[...])

    @pl.when(pl.program_id(0) == 0)
    def _():
        carry_ref[...] = jnp.zeros_like(carry_ref)

    hi, mid, lo = _split3(lf)
    c = _dot(l3_ref[...], jnp.concatenate([hi, mid, lo], axis=0)) + carry_ref[...]
    carry_ref[...] = c[tm - 1:tm, :]
    c2 = c * LOG2E
    hi, mid, lo = _split3(c2)
    kaug = (_dot(xn, wkaug_ref[...]) + _dot(jnp.concatenate([hi, mid, lo], axis=1), pk_ref[...])
            + kc_ref[...])
    kaug_ref[...] = kaug.astype(BF16)
    hi, mid, lo = _split3(jnp.transpose(c2))
    qaug = (_dot_nt(wqaugT_ref[...], xn) * (LOG2E * FOX_HEAD_DIM ** -0.5)
            + _dot(pq_ref[...], jnp.concatenate([hi, mid, lo], axis=0)) + qc_ref[...])
    qaug_ref[...] = qaug.astype(BF16)


def _inproj_prompt(x, g, w):
    T, D = x.shape
    tm = min(PROJ_TILE, T)
    assert T % tm == 0
    tri = np.tril(np.ones((tm, tm), np.float32))
    l3 = jnp.asarray(np.concatenate([tri] * N_SPLIT, axis=1), BF16)
    pk, pq, kc, qc = _aug_tables()
    ins = (x, g, w['kT'], w['vT'], w['kaug'], w['qaugT'], w['ff'], w['bf'], w['h'], l3, pk, pq, kc, qc)
    row = lambda width: pl.BlockSpec((tm, width), lambda i: (i, 0))
    col = lambda rows: pl.BlockSpec((rows, tm), lambda i: (0, i))
    full = lambda a: pl.BlockSpec(a.shape, lambda i: (0,) * a.ndim)
    sds = jax.ShapeDtypeStruct
    out_shape = (sds((FOX_WIDTH, T), F32), sds((FOX_WIDTH, T), F32), sds((FOX_WIDTH, T), BF16),
                 sds((T, AUG_WIDTH), BF16), sds((AUG_WIDTH, T), BF16), sds((FOX_HEADS, T), F32),
                 sds((T, 3 * HG_WIDTH), F32))
    out_specs = (col(FOX_WIDTH), col(FOX_WIDTH), col(FOX_WIDTH), row(AUG_WIDTH), col(AUG_WIDTH),
                 col(FOX_HEADS), row(3 * HG_WIDTH))
    return pl.pallas_call(
        functools.partial(_inproj_prompt_kernel, tm=tm),
        grid=(T // tm,), in_specs=[row(D)] + [full(a) for a in ins[1:]], out_specs=out_specs,
        out_shape=out_shape, scratch_shapes=[pltpu.VMEM((1, LANES), F32)],
        compiler_params=_cparams("arbitrary"), name="inproj_prompt")(*ins)


def _inproj_sample_kernel(x_ref, g_ref, wqkv_ref, wff_ref, bf_ref, wh_ref, wmq_ref,
                          q_ref, k_ref, v_ref, lf_ref, h3_ref, mq_ref):
    xn = _rms_rows(x_ref[...], g_ref[...]).astype(BF16)
    qkv = _dot(xn, wqkv_ref[...])
    q_ref[...] = qkv[:, :FOX_WIDTH] * FOX_HEAD_DIM ** -0.5
    k_ref[...] = qkv[:, FOX_WIDTH:2 * FOX_WIDTH]
    v_ref[...] = qkv[:, 2 * FOX_WIDTH:]
    lf = _log_sigmoid(_dot(xn, wff_ref[...]) + bf_ref[...])
    lf_ref[...] = lf[:, :FOX_HEADS]
    h3_ref[...] = _dot(xn, wh_ref[...])
    mq_ref[...] = _dot(xn, wmq_ref[...])


def _inproj_sample(x, g, w):
    n = x.shape[0]
    sds = lambda width: jax.ShapeDtypeStruct((n, width), F32)
    return pl.pallas_call(
        _inproj_sample_kernel,
        out_shape=(sds(FOX_WIDTH), sds(FOX_WIDTH), sds(FOX_WIDTH), sds(FOX_HEADS), sds(3 * HG_WIDTH),
                   sds(MEM_WIDTH)),
        compiler_params=pltpu.CompilerParams(vmem_limit_bytes=VMEM_LIMIT),
        name="inproj_sample")(x, g, w['qkv'], w['ff'], w['bf'], w['h'], w['mq'])


def _fox_prompt_kernel(qi_ref, kj_ref, qaug_ref, kaug_ref, vT_ref, o_ref, acc_ref, m_ref, l_ref, s_ref,
                       *, tile):
    s = pl.program_id(0)
    qi = qi_ref[s]
    kj = kj_ref[s]
    hd = FOX_HEAD_DIM

    @pl.when(kj == 0)
    def _():
        acc_ref[...] = jnp.zeros_like(acc_ref)
        m_ref[...] = jnp.full_like(m_ref, NEG_BIG)
        l_ref[...] = jnp.zeros_like(l_ref)

    def step(diag):
        if diag:
            causal = (lax.broadcasted_iota(jnp.int32, (tile, tile), 0)
                      <= lax.broadcasted_iota(jnp.int32, (tile, tile), 1))
        nb = FOX_LOOKAHEAD + 1

        def scores(h):
            sT = _dot(kaug_ref[:, h * AUG:(h + 1) * AUG], qaug_ref[h * AUG:(h + 1) * AUG, :])
            if diag:
                sT = jnp.where(causal, sT, NEG_BIG)
            s_ref[h % nb] = sT
            return jnp.max(sT, axis=0, keepdims=True)

        queue = [scores(h) for h in range(FOX_LOOKAHEAD)]
        for h in range(FOX_HEADS):
            mx = queue.pop(0)
            if h + FOX_LOOKAHEAD < FOX_HEADS:
                queue.append(scores(h + FOX_LOOKAHEAD))
            m_old = m_ref[h:h + 1, :]
            m_new = jnp.maximum(m_old, mx)
            alpha = jnp.exp2(m_old - m_new)
            pT = jnp.exp2(s_ref[h % nb] - m_new)
            l_ref[h:h + 1, :] = alpha * l_ref[h:h + 1, :] + jnp.sum(pT, axis=0, keepdims=True)
            m_ref[h:h + 1, :] = m_new
            pv = _dot(vT_ref[h * hd:(h + 1) * hd, :], pT.astype(BF16))
            acc_ref[h * hd:(h + 1) * hd, :] = alpha * acc_ref[h * hd:(h + 1) * hd, :] + pv

    @pl.when(kj < qi)
    def _():
        step(False)

    @pl.when(kj == qi)
    def _():
        step(True)
        inv = 1.0 / l_ref[...]
        for h in range(FOX_HEADS):
            acc_ref[h * hd:(h + 1) * hd, :] = acc_ref[h * hd:(h + 1) * hd, :] * inv[h:h + 1, :]
        o_ref[...] = jnp.transpose(acc_ref[...])


def _fox_prompt(qaug, kaug, vTb):
    T = kaug.shape[0]
    tile = min(FOX_TILE, T)
    assert T % tile == 0
    n = T // tile
    pairs = [(i, j) for i in range(n) for j in range(i + 1)]
    qi = jnp.asarray([p[0] for p in pairs], jnp.int32)
    kj = jnp.asarray([p[1] for p in pairs], jnp.int32)
    grid_spec = pltpu.PrefetchScalarGridSpec(
        num_scalar_prefetch=2, grid=(len(pairs),),
        in_specs=[
            pl.BlockSpec((AUG_WIDTH, tile), lambda s, qi, kj: (0, qi[s])),
            pl.BlockSpec((tile, AUG_WIDTH), lambda s, qi, kj: (kj[s], 0)),
            pl.BlockSpec((FOX_WIDTH, tile), lambda s, qi, kj: (0, kj[s])),
        ],
        out_specs=pl.BlockSpec((tile, FOX_WIDTH), lambda s, qi, kj: (qi[s], 0)),
        scratch_shapes=[pltpu.VMEM((FOX_WIDTH, tile), F32), pltpu.VMEM((FOX_HEADS, tile), F32),
                        pltpu.VMEM((FOX_HEADS, tile), F32),
                        pltpu.VMEM((FOX_LOOKAHEAD + 1, tile, tile), F32)])
    return pl.pallas_call(
        functools.partial(_fox_prompt_kernel, tile=tile),
        grid_spec=grid_spec, out_shape=jax.ShapeDtypeStruct((T, FOX_WIDTH), F32),
        compiler_params=_cparams("arbitrary"), name="fox_prompt")(qi, kj, qaug, kaug, vTb)


def _hgrn_tables(C):
    levels = []
    B = C
    while B >= 2:
        levels.append(B)
        B //= 2
    nl = len(levels)
    t = np.arange(C)
    coef = np.zeros((nl + 2, C, C), np.float32)
    mask = np.zeros((nl + 1, C, C), np.float32)
    for li, B in enumerate(levels):
        h = B // 2
        pos = t % B
        ref = t - pos + h - 1
        for i in range(C):
            if pos[i] >= h:
                coef[li, i, ref[i] + 1:i + 1] = 1.0
            else:
                coef[li, i, i + 1:ref[i] + 1] = 1.0
        same = (t[:, None] // B) == (t[None, :] // B)
        mask[li] = same & (pos[:, None] >= h) & (pos[None, :] < h)
    mask[nl] = np.eye(C)
    coef[nl] = np.tril(np.ones((C, C)))
    coef[nl + 1] = np.triu(np.ones((C, C)), 1)
    coef = coef.reshape((nl + 2) * C, C)
    coef3 = np.concatenate([coef] * N_SPLIT, axis=1)
    return nl, jnp.asarray(coef3, BF16), jnp.asarray(mask, F32)


def _hgrn_prompt_kernel(h3_ref, lbp_ref, g_ref, coef_ref, mask_ref, o_ref, s_out_ref, st_ref,
                        *, C, nl, layer, n_slots):
    i = pl.program_id(0)
    dk = HG_HEAD_DIM

    @pl.when(i == 0)
    def _():
        st_ref[...] = jnp.zeros_like(st_ref)

    lb = _lower_bound([lbp_ref[j:j + 1, :] for j in range(n_slots)], layer)
    hq = h3_ref[:, :HG_WIDTH]
    hf = h3_ref[:, HG_WIDTH:2 * HG_WIDTH]
    v_all = h3_ref[:, 2 * HG_WIDTH:]
    q_all = hq * _sigmoid(hq)
    f = lb + (1.0 - lb) * _sigmoid(hf)
    k_all = 1.0 - f
    hi, mid, lo = _split3(jnp.log(f))
    d_all = _dot(coef_ref[...], jnp.concatenate([hi, mid, lo], axis=0))

    for h in range(HG_HEADS):
        cs = slice(h * dk, (h + 1) * dk)
        q, k, v = q_all[:, cs], k_all[:, cs], v_all[:, cs]
        vb = v.astype(BF16)
        att = mask_ref[nl] * _dot_nt(q.astype(BF16), k.astype(BF16))
        for li in range(nl):
            e = jnp.exp(d_all[li * C:(li + 1) * C, cs])
            att = att + mask_ref[li] * _dot_nt((q * e).astype(BF16), (k * e).astype(BF16))
        st = st_ref[h]
        eb = jnp.exp(d_all[nl * C:(nl + 1) * C, cs])
        o = _dot(att.astype(BF16), vb) + _dot_nt((q * eb).astype(BF16), st.astype(BF16))
        ek = jnp.exp(d_all[(nl + 1) * C:(nl + 2) * C, cs])
        st_new = st * eb[C - 1:C, :] + _dot_tn(vb, (k * ek).astype(BF16))
        st_ref[h] = st_new
        o_ref[:, cs] = _rms_rows(o, g_ref[:, cs])

    @pl.when(i == pl.num_programs(0) - 1)
    def _():
        for h in range(HG_HEADS):
            s_out_ref[h] = jnp.transpose(st_ref[h])


def _hgrn_prompt(h3, lb_param, hg_g, layer):
    T = h3.shape[0]
    C = min(HG_CHUNK, T)
    assert T % C == 0
    nl, coef3, mask = _hgrn_tables(C)
    n_slots = lb_param.shape[0]
    full = lambda a: pl.BlockSpec(a.shape, lambda i: (0,) * a.ndim)
    return pl.pallas_call(
        functools.partial(_hgrn_prompt_kernel, C=C, nl=nl, layer=layer, n_slots=n_slots),
        grid=(T // C,),
        in_specs=[pl.BlockSpec((C, 3 * HG_WIDTH), lambda i: (i, 0)), full(lb_param), full(hg_g),
                  full(coef3), full(mask)],
        out_specs=(pl.BlockSpec((C, HG_WIDTH), lambda i: (i, 0)),
                   pl.BlockSpec((HG_HEADS, HG_HEAD_DIM, HG_HEAD_DIM), lambda i: (0, 0, 0))),
        out_shape=(jax.ShapeDtypeStruct((T, HG_WIDTH), F32),
                   jax.ShapeDtypeStruct((HG_HEADS, HG_HEAD_DIM, HG_HEAD_DIM), F32)),
        scratch_shapes=[pltpu.VMEM((HG_HEADS, HG_HEAD_DIM, HG_HEAD_DIM), F32)],
        compiler_params=_cparams("arbitrary"), name="hgrn_prompt")(h3, lb_param, hg_g, coef3, mask)


def _mem_kv_kernel(mem_ref, g_ref, wk_ref, wv_ref, mk_ref, mv_ref):
    mn = _rms_rows(mem_ref[...], g_ref[...]).astype(BF16)
    mk_ref[...] = _dot(mn, wk_ref[...])
    mv_ref[...] = _dot(mn, wv_ref[...])


def _mem_kv(mem, g, wk, wv):
    n = mem.shape[0]
    sds = jax.ShapeDtypeStruct((n, MEM_WIDTH), F32)
    return pl.pallas_call(_mem_kv_kernel, out_shape=(sds, sds),
                          compiler_params=pltpu.CompilerParams(vmem_limit_bytes=VMEM_LIMIT),
                          name="mem_kv")(mem, g, wk, wv)


def _merge_kernel(*refs, shared_mem, final):
    if shared_mem:
        (x_ref, g_ref, ofox_ref, ohg_ref, wmq_ref, mk_ref, mv_ref,
         wz_ref, wg_ref, wpa_ref, wpb_ref, wpm_ref, wo_ref, fg_ref, y_ref) = refs
    else:
        (x_ref, g_ref, ofox_ref, ohg_ref, omem_ref,
         wz_ref, wg_ref, wpa_ref, wpb_ref, wpm_ref, wo_ref, fg_ref, y_ref) = refs
    x = x_ref[...]
    xn = _rms_rows(x, g_ref[...]).astype(BF16)

    if shared_mem:
        mq = _dot(xn, wmq_ref[...])
        scale = MEM_HEAD_DIM ** -0.5
        heads = []
        for h in range(MEM_HEADS):
            cs = slice(h * MEM_HEAD_DIM, (h + 1) * MEM_HEAD_DIM)
            sc = _dot_nt(mq[:, cs].astype(BF16), mk_ref[:, cs]) * scale
            p = jnp.exp(sc - jnp.max(sc, axis=-1, keepdims=True))
            p = p / jnp.sum(p, axis=-1, keepdims=True)
            heads.append(_dot(p.astype(BF16), mv_ref[:, cs]))
        o_mem = jnp.concatenate(heads, axis=-1)
    else:
        o_mem = omem_ref[...]

    z = _dot(xn, wz_ref[...])
    sz = z * _sigmoid(z)
    ya = _dot((ofox_ref[...] * sz[:, :FOX_WIDTH]).astype(BF16), wpa_ref[...])
    yb = _dot((ohg_ref[...] * sz[:, FOX_WIDTH:FOX_WIDTH + HG_WIDTH]).astype(BF16), wpb_ref[...])
    ym = _dot((o_mem * sz[:, FOX_WIDTH + HG_WIDTH:]).astype(BF16), wpm_ref[...])
    d = x.shape[1]
    gates = _sigmoid(_dot(xn, wg_ref[...]))
    mixed = gates[:, :d] * ya + gates[:, d:2 * d] * yb + gates[:, 2 * d:] * ym
    out = x + _dot(mixed.astype(BF16), wo_ref[...])
    y_ref[...] = _rms_rows(out, fg_ref[...]) if final else out


def _merge(x, g, o_fox, o_hg, w, final_g, *, final, mem_kv=None, o_mem=None):
    T, D = x.shape
    shared = mem_kv is not None
    tm = min(PROJ_TILE, T)
    assert T % tm == 0
    row = lambda width: pl.BlockSpec((tm, width), lambda i: (i, 0))
    full = lambda a: pl.BlockSpec(a.shape, lambda i: (0,) * a.ndim)
    tail = (w['z'], w['g'], w['pa'], w['pb'], w['pm'], w['o'], final_g)
    if shared:
        ins = (x, g, o_fox, o_hg, w['mq'], mem_kv[0], mem_kv[1]) + tail
        in_specs = [row(D), full(g), row(FOX_WIDTH), row(HG_WIDTH)] + [full(a) for a in ins[4:]]
    else:
        ins = (x, g, o_fox, o_hg, o_mem) + tail
        in_specs = ([row(D), full(g), row(FOX_WIDTH), row(HG_WIDTH), row(MEM_WIDTH)]
                    + [full(a) for a in ins[5:]])
    return pl.pallas_call(
        functools.partial(_merge_kernel, shared_mem=shared, final=final),
        grid=(T // tm,), in_specs=in_specs, out_specs=row(D),
        out_shape=jax.ShapeDtypeStruct((T, D), F32),
        compiler_params=_cparams("arbitrary"),
        name="merge_prompt" if shared else "merge_sample")(*ins)


def _fox_decode_kernel(*refs, G, page):
    k_refs = refs[1:1 + G]
    v_refs = refs[1 + G:1 + 2 * G]
    lf_refs = refs[1 + 2 * G:1 + 3 * G]
    (q_ref, kn_ref, vn_ref, cn_ref, o_ref, qb_ref, acc_ref, m_ref, l_ref, carry_ref) = refs[1 + 3 * G:]
    step = pl.program_id(1)
    hd = FOX_HEAD_DIM

    @pl.when(step == 0)
    def _():
        qb_ref[...] = jnp.broadcast_to(q_ref[...], qb_ref.shape)
        acc_ref[...] = jnp.zeros_like(acc_ref)
        m_ref[...] = jnp.full_like(m_ref, NEG_BIG)
        l_ref[...] = jnp.zeros_like(l_ref)
        carry_ref[...] = jnp.zeros_like(carry_ref)

    lane = lax.broadcasted_iota(jnp.int32, (FOX_HEADS, page), 1)
    later = carry_ref[...]
    cn = cn_ref[...]
    bias = []
    for g in range(G):
        lf = lf_refs[g][...]
        suf = lf
        sh = 1
        while sh < page:
            suf = suf + jnp.where(lane + sh < page, pltpu.roll(suf, page - sh, 1), 0.0)
            sh *= 2
        bias.append((suf - lf) + (later + cn))
        later = later + suf[:, 0:1]
    carry_ref[...] = later

    for h in range(FOX_HEADS):
        rows = slice(h * hd, (h + 1) * hd)
        qb = qb_ref[rows, :]
        s = [jnp.sum(k_refs[g][rows, :] * qb, axis=0, keepdims=True) + bias[g][h:h + 1, :] for g in range(G)]
        m_old = m_ref[h:h + 1, :]
        m_new = m_old
        for sg in s:
            m_new = jnp.maximum(m_new, jnp.max(sg, axis=1, keepdims=True))
        alpha = jnp.exp(m_old - m_new)
        p = [jnp.exp(sg - m_new) for sg in s]
        l_new = alpha * l_ref[h:h + 1, :]
        upd = alpha * acc_ref[rows, :]
        for g in range(G):
            l_new = l_new + jnp.sum(p[g], axis=1, keepdims=True)
            upd = upd + v_refs[g][rows, :] * p[g]
        l_ref[h:h + 1, :] = l_new
        m_ref[h:h + 1, :] = m_new
        acc_ref[rows, :] = upd

    @pl.when(step == pl.num_programs(1) - 1)
    def _():
        for h in range(FOX_HEADS):
            rows = slice(h * hd, (h + 1) * hd)
            s_new = jnp.sum(q_ref[rows, :] * kn_ref[rows, :], axis=0, keepdims=True)
            m_old = m_ref[h:h + 1, :]
            m_new = jnp.maximum(m_old, s_new)
            alpha = jnp.exp(m_old - m_new)
            p_new = jnp.exp(s_new - m_new)
            l = alpha * l_ref[h:h + 1, :] + p_new
            o = alpha * jnp.sum(acc_ref[rows, :], axis=1, keepdims=True) + p_new * vn_ref[rows, :]
            o_ref[rows, :] = o / l


def _fox_decode(page_table, cache_kT, cache_vT, cache_lfT, q_col, kn_col, vn_col, cn_col, layer):
    DB, n_pages = page_table.shape
    page = cache_kT.shape[-1]
    G = min(PAGES_PER_STEP, n_pages)
    assert n_pages % G == 0

    def page_map(g):
        def index_map(b, s, pt):
            return (layer, pt[b, n_pages - 1 - (s * G + g)], 0, 0)
        return index_map

    kv_specs = [pl.BlockSpec((None, None, FOX_WIDTH, page), page_map(g)) for g in range(G)]
    lf_specs = [pl.BlockSpec((None, None, FOX_HEADS, page), page_map(g)) for g in range(G)]
    per_b = lambda rows: pl.BlockSpec((None, rows, 1), lambda b, s, pt: (b, 0, 0))
    grid_spec = pltpu.PrefetchScalarGridSpec(
        num_scalar_prefetch=1, grid=(DB, n_pages // G),
        in_specs=kv_specs + kv_specs + lf_specs + [per_b(FOX_WIDTH)] * 3 + [per_b(FOX_HEADS)],
        out_specs=per_b(FOX_WIDTH),
        scratch_shapes=[pltpu.VMEM((FOX_WIDTH, page), F32), pltpu.VMEM((FOX_WIDTH, page), F32),
                        pltpu.VMEM((FOX_HEADS, 1), F32), pltpu.VMEM((FOX_HEADS, 1), F32),
                        pltpu.VMEM((FOX_HEADS, 1), F32)])
    return pl.pallas_call(
        functools.partial(_fox_decode_kernel, G=G, page=page),
        grid_spec=grid_spec, out_shape=jax.ShapeDtypeStruct((DB, FOX_WIDTH, 1), F32),
        compiler_params=_cparams("arbitrary", "arbitrary"), name="fox_decode")(
            page_table, *([cache_kT] * G), *([cache_vT] * G), *([cache_lfT] * G),
            q_col, kn_col, vn_col, cn_col)


def _sample_mix_kernel(st_ref, hq_ref, hf_ref, hi_ref, lbp_ref, g_ref, mq_ref, mk_ref, mv_ref,
                       st_out_ref, ohg_ref, omem_ref, *, layer, n_slots):
    for h in range(HG_HEADS):
        lb = _lower_bound([lbp_ref[j, h] for j in range(n_slots)], layer)
        hq = hq_ref[h]
        q = hq * _sigmoid(hq)
        f = lb + (1.0 - lb) * _sigmoid(hf_ref[h])
        s_new = f * st_ref[h] + (1.0 - f) * hi_ref[h]
        st_out_ref[h] = s_new
        o = jnp.sum(s_new * q, axis=0, keepdims=True)
        ohg_ref[h] = _rms_rows(o, g_ref[h])

    q = mq_ref[...] * (MEM_HEAD_DIM ** -0.5)
    q8 = jnp.concatenate([q, jnp.zeros_like(q)], axis=0).astype(BF16)
    s = _dot_nt(q8, mk_ref[...].astype(BF16))
    own = (lax.broadcasted_iota(jnp.int32, s.shape, 1) % MEM_HEADS
           == lax.broadcasted_iota(jnp.int32, s.shape, 0))
    s = jnp.where(own, s, NEG_BIG)
    p = jnp.exp(s - jnp.max(s, axis=-1, keepdims=True))
    o = _dot(p.astype(BF16), mv_ref[...].astype(BF16)) / jnp.sum(p, axis=-1, keepdims=True)
    omem_ref[...] = o[:MEM_HEADS, :]


def _sample_mix(state, hq_col, hf_col, hi_row, lbp_col, hg_g_row, mq, mk_flat, mv_flat, layer):
    DB = state.shape[0]
    n_slots = lbp_col.shape[0]
    per_b = lambda a: pl.BlockSpec((None,) + a.shape[1:], lambda b: (b,) + (0,) * (a.ndim - 1))
    full = lambda a: pl.BlockSpec(a.shape, lambda b: (0,) * a.ndim)
    ins = (state, hq_col, hf_col, hi_row, lbp_col, hg_g_row, mq, mk_flat, mv_flat)
    in_specs = [per_b(state), per_b(hq_col), per_b(hf_col), per_b(hi_row), full(lbp_col), full(hg_g_row),
                per_b(mq), per_b(mk_flat), per_b(mv_flat)]
    out_shape = (jax.ShapeDtypeStruct(state.shape, F32),
                 jax.ShapeDtypeStruct((DB, HG_HEADS, 1, HG_HEAD_DIM), F32),
                 jax.ShapeDtypeStruct((DB, MEM_HEADS, MEM_HEAD_DIM), F32))
    out_specs = tuple(pl.BlockSpec((None,) + s.shape[1:], lambda b, n=len(s.shape): (b,) + (0,) * (n - 1))
                      for s in out_shape)
    return pl.pallas_call(
        functools.partial(_sample_mix_kernel, layer=layer, n_slots=n_slots),
        grid=(DB,), in_specs=in_specs, out_specs=out_specs, out_shape=out_shape,
        compiler_params=_cparams("arbitrary"), name="sample_mix")(*ins)


def _layer_weights(w_in, b_fgate, w_mk, w_mv, w_pa, w_pb, w_pm, w_o):
    fw, hw, mw = FOX_WIDTH, HG_WIDTH, MEM_WIDTH
    d = w_in.shape[0]
    o = 0
    qkv = w_in[:, o:o + 3 * fw]; o += 3 * fw
    ff = w_in[:, o:o + FOX_HEADS]; o += FOX_HEADS
    fz = w_in[:, o:o + fw]; o += fw
    h3 = w_in[:, o:o + 3 * hw]; o += 3 * hw
    hz = w_in[:, o:o + hw]; o += hw
    mq = w_in[:, o:o + mw]; o += mw
    mz = w_in[:, o:o + mw]; o += mw
    gates = w_in[:, o:o + 3 * d]; o += 3 * d
    assert o == w_in.shape[1]
    pad_heads = AUG - FOX_HEAD_DIM
    wT = w_in.T[:3 * fw].astype(BF16)
    qaugT = jnp.pad(wT[:fw].reshape(FOX_HEADS, FOX_HEAD_DIM, d), ((0, 0), (0, pad_heads), (0, 0)))
    kaug = jnp.pad(qkv[:, fw:2 * fw].astype(BF16).reshape(d, FOX_HEADS, FOX_HEAD_DIM),
                   ((0, 0), (0, 0), (0, pad_heads)))
    return dict(
        qkv=qkv.astype(BF16), kT=wT[fw:2 * fw], vT=wT[2 * fw:],
        qaugT=qaugT.reshape(AUG_WIDTH, d), kaug=kaug.reshape(d, AUG_WIDTH),
        ff=jnp.pad(ff, ((0, 0), (0, LANES - FOX_HEADS))).astype(BF16),
        bf=jnp.pad(b_fgate.astype(F32), (0, LANES - FOX_HEADS)).reshape(1, LANES),
        h=h3.astype(BF16), mq=mq.astype(BF16),
        z=jnp.concatenate([fz, hz, mz], axis=1).astype(BF16), g=gates.astype(BF16),
        mk=w_mk.astype(BF16), mv=w_mv.astype(BF16),
        pa=w_pa.astype(BF16), pb=w_pb.astype(BF16), pm=w_pm.astype(BF16), o=w_o.astype(BF16))


def kernel(x_prompt, x_sample, mem_prompt, cache_fox_k, cache_fox_v, cache_fox_logf, page_table, state_hgrn, cache_mem_k, cache_mem_v, norm_g, w_in, b_fgate, hg_norm_g, lb_param, mem_norm_g, w_mk, w_mv, w_pa, w_pb, w_pm, w_o, final_norm_g):
    depth = w_in.shape[0]
    B, T, D = x_prompt.shape
    DB, Tn, _ = x_sample.shape
    assert B == 1 and Tn == 1
    n_phys, page = cache_fox_k.shape[1], cache_fox_k.shape[2]
    n_mem = cache_mem_k.shape[2]
    n_slots = lb_param.shape[0]

    xp = x_prompt.reshape(T, D)
    xs = x_sample.reshape(DB, D)
    mem = mem_prompt.reshape(n_mem, D)
    cache_kT = jnp.transpose(cache_fox_k, (0, 1, 3, 4, 2)).reshape(depth, n_phys, FOX_WIDTH, page)
    cache_vT = jnp.transpose(cache_fox_v, (0, 1, 3, 4, 2)).reshape(depth, n_phys, FOX_WIDTH, page)
    cache_lfT = jnp.transpose(cache_fox_logf, (0, 1, 3, 2))
    mk_cache = cache_mem_k.reshape(depth, DB, n_mem * MEM_HEADS, MEM_HEAD_DIM)
    mv_cache = cache_mem_v.reshape(depth, DB, n_mem * MEM_HEADS, MEM_HEAD_DIM)
    lbp_col = lb_param.reshape(n_slots, HG_HEADS, HG_HEAD_DIM, 1)
    final_g = final_norm_g.reshape(1, D)
    row = lambda a: a.reshape(1, -1)
    tok_major = lambda aT: jnp.transpose(aT.reshape(FOX_HEADS, FOX_HEAD_DIM, T), (2, 0, 1))

    outs = [[] for _ in range(10)]
    for l in range(depth):
        w = _layer_weights(w_in[l], b_fgate[l], w_mk[l], w_mv[l], w_pa[l], w_pb[l], w_pm[l], w_o[l])
        g = row(norm_g[l])
        last = l == depth - 1

        kT, vT, vTb, kaug, qaug, lfT, h3 = _inproj_prompt(xp, g, w)
        o_fox = _fox_prompt(qaug, kaug, vTb)
        o_hg, s_p = _hgrn_prompt(h3, lb_param, row(hg_norm_g[l]), l)
        mk, mv = _mem_kv(mem, row(mem_norm_g[l]), w['mk'], w['mv'])
        xp = _merge(xp, g, o_fox, o_hg, w, final_g, final=last,
                    mem_kv=(mk.astype(BF16), mv.astype(BF16)))

        qs, ks, vs, lfs, h3s, mqs = _inproj_sample(xs, g, w)
        o_fox_s = _fox_decode(page_table, cache_kT, cache_vT, cache_lfT, qs.reshape(DB, FOX_WIDTH, 1),
                              ks.reshape(DB, FOX_WIDTH, 1), vs.reshape(DB, FOX_WIDTH, 1),
                              lfs.reshape(DB, FOX_HEADS, 1), l)
        col = lambda a: a.reshape(DB, HG_HEADS, HG_HEAD_DIM, 1)
        s_s, o_hg_s, o_mem_s = _sample_mix(
            state_hgrn[l], col(h3s[:, :HG_WIDTH]), col(h3s[:, HG_WIDTH:2 * HG_WIDTH]),
            h3s[:, 2 * HG_WIDTH:].reshape(DB, HG_HEADS, 1, HG_HEAD_DIM), lbp_col,
            hg_norm_g[l].reshape(HG_HEADS, 1, HG_HEAD_DIM),
            mqs.reshape(DB, MEM_HEADS, MEM_HEAD_DIM), mk_cache[l], mv_cache[l], l)
        xs = _merge(xs, g, o_fox_s.reshape(DB, FOX_WIDTH), o_hg_s.reshape(DB, HG_WIDTH), w, final_g,
                    final=last, o_mem=o_mem_s.reshape(DB, MEM_WIDTH))

        new = (tok_major(kT).reshape(B, T, FOX_HEADS, FOX_HEAD_DIM),
               tok_major(vT).reshape(B, T, FOX_HEADS, FOX_HEAD_DIM),
               jnp.transpose(lfT).reshape(B, T, FOX_HEADS), s_p.reshape(B, HG_HEADS, HG_HEAD_DIM, HG_HEAD_DIM),
               mk.reshape(B, n_mem, MEM_HEADS, MEM_HEAD_DIM), mv.reshape(B, n_mem, MEM_HEADS, MEM_HEAD_DIM),
               ks.reshape(DB, Tn, FOX_HEADS, FOX_HEAD_DIM), vs.reshape(DB, Tn, FOX_HEADS, FOX_HEAD_DIM),
               lfs.reshape(DB, Tn, FOX_HEADS), s_s)
        for acc, a in zip(outs, new):
            acc.append(a)

    return (xp.reshape(B, T, D), xs.reshape(DB, Tn, D)) + tuple(jnp.stack(a) for a in outs)
```

```python
import functools

import numpy as np
import jax
import jax.numpy as jnp
from jax import lax
from jax.experimental import pallas as pl
from jax.experimental.pallas import tpu as pltpu

F32 = jnp.float32
BF16 = jnp.bfloat16

FOX_HEADS = 8
FOX_HEAD_DIM = 64
FOX_WIDTH = FOX_HEADS * FOX_HEAD_DIM
HG_HEADS = 4
HG_HEAD_DIM = 128
HG_WIDTH = HG_HEADS * HG_HEAD_DIM
MEM_HEADS = 4
MEM_HEAD_DIM = 128
MEM_WIDTH = MEM_HEADS * MEM_HEAD_DIM
RMS_EPS = 1e-6
NEG_BIG = -1e30
LOG2E = 1.4426950408889634

LANES = 128
VMEM_LIMIT = 52 * 1024 * 1024

PROJ_TILE = 256
FOX_TILE = 512
FOX_LOOKAHEAD = 3
HG_CHUNK = 128
PAGES_PER_STEP = 16

AUG = 2 * FOX_HEAD_DIM
AUG_WIDTH = FOX_HEADS * AUG
N_SPLIT = 3


def _cparams(*sem):
    return pltpu.CompilerParams(dimension_semantics=sem, vmem_limit_bytes=VMEM_LIMIT)


def _sigmoid(x):
    return 1.0 / (1.0 + jnp.exp(-x))


def _split3(x):
    hi = x.astype(BF16)
    r1 = x - hi.astype(F32)
    mid = r1.astype(BF16)
    lo = (r1 - mid.astype(F32)).astype(BF16)
    return hi, mid, lo


def _dot(a, b):
    return jnp.dot(a, b, preferred_element_type=F32)


def _dot_nt(a, b):
    return lax.dot_general(a, b, (((1,), (1,)), ((), ())), preferred_element_type=F32)


def _dot_tn(a, b):
    return lax.dot_general(a, b, (((0,), (0,)), ((), ())), preferred_element_type=F32)


def _rms_rows(x, g):
    ms = jnp.mean(x * x, axis=-1, keepdims=True)
    return x * lax.rsqrt(ms + RMS_EPS) * g


def _lower_bound(lbp, layer):
    m = lbp[0]
    for p in lbp[1:]:
        m = jnp.maximum(m, p)
    e = [jnp.exp(p - m) for p in lbp]
    tot = e[0]
    for t in e[1:]:
        tot = tot + t
    part = e[0]
    for t in e[1:layer + 1]:
        part = part + t
    return part / tot


def _aug_tables():
    pk = np.zeros((N_SPLIT * LANES, AUG_WIDTH), np.float32)
    pq = np.zeros((AUG_WIDTH, N_SPLIT * LANES), np.float32)
    kc = np.zeros((1, AUG_WIDTH), np.float32)
    qc = np.zeros((AUG_WIDTH, 1), np.float32)
    for h in range(FOX_HEADS):
        for s in range(N_SPLIT):
            pk[s * LANES + h, h * AUG + FOX_HEAD_DIM + s] = -1.0
            qc[h * AUG + FOX_HEAD_DIM + s, 0] = 1.0
            kc[0, h * AUG + FOX_HEAD_DIM + N_SPLIT + s] = 1.0
            pq[h * AUG + FOX_HEAD_DIM + N_SPLIT + s, s * LANES + h] = 1.0
    return jnp.asarray(pk, BF16), jnp.asarray(pq, BF16), jnp.asarray(kc), jnp.asarray(qc)


def _log_sigmoid(x):
    return jnp.minimum(x, 0.0) - jnp.log1p(jnp.exp(-jnp.abs(x)))


def _inproj_prompt_kernel(x_ref, g_ref, wkT_ref, wvT_ref, wkaug_ref, wqaugT_ref, wff_ref, bf_ref, wh_ref,
                          l3_ref, pk_ref, pq_ref, kc_ref, qc_ref,
                          kT_ref, vT_ref, vTb_ref, kaug_ref, qaug_ref, lfT_ref, h3_ref, carry_ref, *, tm):
    xn = _rms_rows(x_ref[...], g_ref[...]).astype(BF16)
    kT_ref[...] = _dot_nt(wkT_ref[...], xn)
    vT = _dot_nt(wvT_ref[...], xn)
    vT_ref[...] = vT
    vTb_ref[...] = vT.astype(BF16)
    lf = _log_sigmoid(_dot(xn, wff_ref[...]) + bf_ref[...])
    lfT_ref[...] = jnp.transpose(lf)[:FOX_HEADS, :]
    h3_ref[...] = _dot(xn, wh_ref[...])

    @pl.when(pl.program_id(0) == 0)
    def _():
        carry_ref[...] = jnp.zeros_like(carry_ref)

    hi, mid, lo = _split3(lf)
    c = _dot(l3_ref[...], jnp.concatenate([hi, mid, lo], axis=0)) + carry_ref[...]
    carry_ref[...] = c[tm - 1:tm, :]
    c2 = c * LOG2E
    hi, mid, lo = _split3(c2)
    kaug = (_dot(xn, wkaug_ref[...]) + _dot(jnp.concatenate([hi, mid, lo], axis=1), pk_ref[...])
            + kc_ref[...])
    kaug_ref[...] = kaug.astype(BF16)
    hi, mid, lo = _split3(jnp.transpose(c2))
    qaug = (_dot_nt(wqaugT_ref[...], xn) * (LOG2E * FOX_HEAD_DIM ** -0.5)
            + _dot(pq_ref[...], jnp.concatenate([hi, mid, lo], axis=0)) + qc_ref[...])
    qaug_ref[...] = qaug.astype(BF16)


def _inproj_prompt(x, g, w):
    T, D = x.shape
    tm = min(PROJ_TILE, T)
    assert T % tm == 0
    tri = np.tril(np.ones((tm, tm), np.float32))
    l3 = jnp.asarray(np.concatenate([tri] * N_SPLIT, axis=1), BF16)
    pk, pq, kc, qc = _aug_tables()
    ins = (x, g, w['kT'], w['vT'], w['kaug'], w['qaugT'], w['ff'], w['bf'], w['h'], l3, pk, pq, kc, qc)
    row = lambda width: pl.BlockSpec((tm, width), lambda i: (i, 0))
    col = lambda rows: pl.BlockSpec((rows, tm), lambda i: (0, i))
    full = lambda a: pl.BlockSpec(a.shape, lambda i: (0,) * a.ndim)
    sds = jax.ShapeDtypeStruct
    out_shape = (sds((FOX_WIDTH, T), F32), sds((FOX_WIDTH, T), F32), sds((FOX_WIDTH, T), BF16),
                 sds((T, AUG_WIDTH), BF16), sds((AUG_WIDTH, T), BF16), sds((FOX_HEADS, T), F32),
                 sds((T, 3 * HG_WIDTH), F32))
    out_specs = (col(FOX_WIDTH), col(FOX_WIDTH), col(FOX_WIDTH), row(AUG_WIDTH), col(AUG_WIDTH),
                 col(FOX_HEADS), row(3 * HG_WIDTH))
    return pl.pallas_call(
        functools.partial(_inproj_prompt_kernel, tm=tm),
        grid=(T // tm,), in_specs=[row(D)] + [full(a) for a in ins[1:]], out_specs=out_specs,
        out_shape=out_shape, scratch_shapes=[pltpu.VMEM((1, LANES), F32)],
        compiler_params=_cparams("arbitrary"), name="inproj_prompt")(*ins)


def _inproj_sample_kernel(x_ref, g_ref, wqkv_ref, wff_ref, bf_ref, wh_ref, wmq_ref,
                          q_ref, k_ref, v_ref, lf_ref, h3_ref, mq_ref):
    xn = _rms_rows(x_ref[...], g_ref[...]).astype(BF16)
    qkv = _dot(xn, wqkv_ref[...])
    q_ref[...] = qkv[:, :FOX_WIDTH] * FOX_HEAD_DIM ** -0.5
    k_ref[...] = qkv[:, FOX_WIDTH:2 * FOX_WIDTH]
    v_ref[...] = qkv[:, 2 * FOX_WIDTH:]
    lf = _log_sigmoid(_dot(xn, wff_ref[...]) + bf_ref[...])
    lf_ref[...] = lf[:, :FOX_HEADS]
    h3_ref[...] = _dot(xn, wh_ref[...])
    mq_ref[...] = _dot(xn, wmq_ref[...])


def _inproj_sample(x, g, w):
    n = x.shape[0]
    sds = lambda width: jax.ShapeDtypeStruct((n, width), F32)
    return pl.pallas_call(
        _inproj_sample_kernel,
        out_shape=(sds(FOX_WIDTH), sds(FOX_WIDTH), sds(FOX_WIDTH), sds(FOX_HEADS), sds(3 * HG_WIDTH),
                   sds(MEM_WIDTH)),
        compiler_params=pltpu.CompilerParams(vmem_limit_bytes=VMEM_LIMIT),
        name="inproj_sample")(x, g, w['qkv'], w['ff'], w['bf'], w['h'], w['mq'])


def _fox_prompt_kernel(qi_ref, kj_ref, qaug_ref, kaug_ref, vT_ref, o_ref, acc_ref, m_ref, l_ref, s_ref,
                       *, tile):
    s = pl.program_id(0)
    qi = qi_ref[s]
    kj = kj_ref[s]
    hd = FOX_HEAD_DIM

    @pl.when(kj == 0)
    def _():
        acc_ref[...] = jnp.zeros_like(acc_ref)
        m_ref[...] = jnp.full_like(m_ref, NEG_BIG)
        l_ref[...] = jnp.zeros_like(l_ref)

    def step(diag):
        if diag:
            causal = (lax.broadcasted_iota(jnp.int32, (tile, tile), 0)
                      <= lax.broadcasted_iota(jnp.int32, (tile, tile), 1))
        nb = FOX_LOOKAHEAD + 1

        def scores(h):
            sT = _dot(kaug_ref[:, h * AUG:(h + 1) * AUG], qaug_ref[h * AUG:(h + 1) * AUG, :])
            if diag:
                sT = jnp.where(causal, sT, NEG_BIG)
            s_ref[h % nb] = sT
            return jnp.max(sT, axis=0, keepdims=True)

        queue = [scores(h) for h in range(FOX_LOOKAHEAD)]
        for h in range(FOX_HEADS):
            mx = queue.pop(0)
            if h + FOX_LOOKAHEAD < FOX_HEADS:
                queue.append(scores(h + FOX_LOOKAHEAD))
            m_old = m_ref[h:h + 1, :]
            m_new = jnp.maximum(m_old, mx)
            alpha = jnp.exp2(m_old - m_new)
            pT = jnp.exp2(s_ref[h % nb] - m_new)
            l_ref[h:h + 1, :] = alpha * l_ref[h:h + 1, :] + jnp.sum(pT, axis=0, keepdims=True)
            m_ref[h:h + 1, :] = m_new
            pv = _dot(vT_ref[h * hd:(h + 1) * hd, :], pT.astype(BF16))
            acc_ref[h * hd:(h + 1) * hd, :] = alpha * acc_ref[h * hd:(h + 1) * hd, :] + pv

    @pl.when(kj < qi)
    def _():
        step(False)

    @pl.when(kj == qi)
    def _():
        step(True)
        inv = 1.0 / l_ref[...]
        for h in range(FOX_HEADS):
            acc_ref[h * hd:(h + 1) * hd, :] = acc_ref[h * hd:(h + 1) * hd, :] * inv[h:h + 1, :]
        o_ref[...] = jnp.transpose(acc_ref[...])


def _fox_prompt(qaug, kaug, vTb):
    T = kaug.shape[0]
    tile = min(FOX_TILE, T)
    assert T % tile == 0
    n = T // tile
    pairs = [(i, j) for i in range(n) for j in range(i + 1)]
    qi = jnp.asarray([p[0] for p in pairs], jnp.int32)
    kj = jnp.asarray([p[1] for p in pairs], jnp.int32)
    grid_spec = pltpu.PrefetchScalarGridSpec(
        num_scalar_prefetch=2, grid=(len(pairs),),
        in_specs=[
            pl.BlockSpec((AUG_WIDTH, tile), lambda s, qi, kj: (0, qi[s])),
            pl.BlockSpec((tile, AUG_WIDTH), lambda s, qi, kj: (kj[s], 0)),
            pl.BlockSpec((FOX_WIDTH, tile), lambda s, qi, kj: (0, kj[s])),
        ],
        out_specs=pl.BlockSpec((tile, FOX_WIDTH), lambda s, qi, kj: (qi[s], 0)),
        scratch_shapes=[pltpu.VMEM((FOX_WIDTH, tile), F32), pltpu.VMEM((FOX_HEADS, tile), F32),
                        pltpu.VMEM((FOX_HEADS, tile), F32),
                        pltpu.VMEM((FOX_LOOKAHEAD + 1, tile, tile), F32)])
    return pl.pallas_call(
        functools.partial(_fox_prompt_kernel, tile=tile),
        grid_spec=grid_spec, out_shape=jax.ShapeDtypeStruct((T, FOX_WIDTH), F32),
        compiler_params=_cparams("arbitrary"), name="fox_prompt")(qi, kj, qaug, kaug, vTb)


def _hgrn_tables(C):
    levels = []
    B = C
    while B >= 2:
        levels.append(B)
        B //= 2
    nl = len(levels)
    t = np.arange(C)
    coef = np.zeros((nl + 2, C, C), np.float32)
    mask = np.zeros((nl + 1, C, C), np.float32)
    for li, B in enumerate(levels):
        h = B // 2
        pos = t % B
        ref = t - pos + h - 1
        for i in range(C):
            if pos[i] >= h:
                coef[li, i, ref[i] + 1:i + 1] = 1.0
            else:
                coef[li, i, i + 1:ref[i] + 1] = 1.0
        same = (t[:, None] // B) == (t[None, :] // B)
        mask[li] = same & (pos[:, None] >= h) & (pos[None, :] < h)
    mask[nl] = np.eye(C)
    coef[nl] = np.tril(np.ones((C, C)))
    coef[nl + 1] = np.triu(np.ones((C, C)), 1)
    coef = coef.reshape((nl + 2) * C, C)
    coef3 = np.concatenate([coef] * N_SPLIT, axis=1)
    return nl, jnp.asarray(coef3, BF16), jnp.asarray(mask, F32)


def _hgrn_prompt_kernel(h3_ref, lbp_ref, g_ref, coef_ref, mask_ref, o_ref, s_out_ref, st_ref,
                        *, C, nl, layer, n_slots):
    i = pl.program_id(0)
    dk = HG_HEAD_DIM

    @pl.when(i == 0)
    def _():
        st_ref[...] = jnp.zeros_like(st_ref)

    lb = _lower_bound([lbp_ref[j:j + 1, :] for j in range(n_slots)], layer)
    hq = h3_ref[:, :HG_WIDTH]
    hf = h3_ref[:, HG_WIDTH:2 * HG_WIDTH]
    v_all = h3_ref[:, 2 * HG_WIDTH:]
    q_all = hq * _sigmoid(hq)
    f = lb + (1.0 - lb) * _sigmoid(hf)
    k_all = 1.0 - f
    hi, mid, lo = _split3(jnp.log(f))
    d_all = _dot(coef_ref[...], jnp.concatenate([hi, mid, lo], axis=0))

    for h in range(HG_HEADS):
        cs = slice(h * dk, (h + 1) * dk)
        q, k, v = q_all[:, cs], k_all[:, cs], v_all[:, cs]
        vb = v.astype(BF16)
        att = mask_ref[nl] * _dot_nt(q.astype(BF16), k.astype(BF16))
        for li in range(nl):
            e = jnp.exp(d_all[li * C:(li + 1) * C, cs])
            att = att + mask_ref[li] * _dot_nt((q * e).astype(BF16), (k * e).astype(BF16))
        st = st_ref[h]
        eb = jnp.exp(d_all[nl * C:(nl + 1) * C, cs])
        o = _dot(att.astype(BF16), vb) + _dot_nt((q * eb).astype(BF16), st.astype(BF16))
        ek = jnp.exp(d_all[(nl + 1) * C:(nl + 2) * C, cs])
        st_new = st * eb[C - 1:C, :] + _dot_tn(vb, (k * ek).astype(BF16))
        st_ref[h] = st_new
        o_ref[:, cs] = _rms_rows(o, g_ref[:, cs])

    @pl.when(i == pl.num_programs(0) - 1)
    def _():
        for h in range(HG_HEADS):
            s_out_ref[h] = jnp.transpose(st_ref[h])


def _hgrn_prompt(h3, lb_param, hg_g, layer):
    T = h3.shape[0]
    C = min(HG_CHUNK, T)
    assert T % C == 0
    nl, coef3, mask = _hgrn_tables(C)
    n_slots = lb_param.shape[0]
    full = lambda a: pl.BlockSpec(a.shape, lambda i: (0,) * a.ndim)
    return pl.pallas_call(
        functools.partial(_hgrn_prompt_kernel, C=C, nl=nl, layer=layer, n_slots=n_slots),
        grid=(T // C,),
        in_specs=[pl.BlockSpec((C, 3 * HG_WIDTH), lambda i: (i, 0)), full(lb_param), full(hg_g),
                  full(coef3), full(mask)],
        out_specs=(pl.BlockSpec((C, HG_WIDTH), lambda i: (i, 0)),
                   pl.BlockSpec((HG_HEADS, HG_HEAD_DIM, HG_HEAD_DIM), lambda i: (0, 0, 0))),
        out_shape=(jax.ShapeDtypeStruct((T, HG_WIDTH), F32),
                   jax.ShapeDtypeStruct((HG_HEADS, HG_HEAD_DIM, HG_HEAD_DIM), F32)),
        scratch_shapes=[pltpu.VMEM((HG_HEADS, HG_HEAD_DIM, HG_HEAD_DIM), F32)],
        compiler_params=_cparams("arbitrary"), name="hgrn_prompt")(h3, lb_param, hg_g, coef3, mask)


def _mem_kv_kernel(mem_ref, g_ref, wk_ref, wv_ref, mk_ref, mv_ref):
    mn = _rms_rows(mem_ref[...], g_ref[...]).astype(BF16)
    mk_ref[...] = _dot(mn, wk_ref[...])
    mv_ref[...] = _dot(mn, wv_ref[...])


def _mem_kv(mem, g, wk, wv):
    n = mem.shape[0]
    sds = jax.ShapeDtypeStruct((n, MEM_WIDTH), F32)
    return pl.pallas_call(_mem_kv_kernel, out_shape=(sds, sds),
                          compiler_params=pltpu.CompilerParams(vmem_limit_bytes=VMEM_LIMIT),
                          name="mem_kv")(mem, g, wk, wv)


def _merge_kernel(*refs, shared_mem, final):
    if shared_mem:
        (x_ref, g_ref, ofox_ref, ohg_ref, wmq_ref, mk_ref, mv_ref,
         wz_ref, wg_ref, wpa_ref, wpb_ref, wpm_ref, wo_ref, fg_ref, y_ref) = refs
    else:
        (x_ref, g_ref, ofox_ref, ohg_ref, omem_ref,
         wz_ref, wg_ref, wpa_ref, wpb_ref, wpm_ref, wo_ref, fg_ref, y_ref) = refs
    x = x_ref[...]
    xn = _rms_rows(x, g_ref[...]).astype(BF16)

    if shared_mem:
        mq = _dot(xn, wmq_ref[...])
        scale = MEM_HEAD_DIM ** -0.5
        heads = []
        for h in range(MEM_HEADS):
            cs = slice(h * MEM_HEAD_DIM, (h + 1) * MEM_HEAD_DIM)
            sc = _dot_nt(mq[:, cs].astype(BF16), mk_ref[:, cs]) * scale
            p = jnp.exp(sc - jnp.max(sc, axis=-1, keepdims=True))
            p = p / jnp.sum(p, axis=-1, keepdims=True)
            heads.append(_dot(p.astype(BF16), mv_ref[:, cs]))
        o_mem = jnp.concatenate(heads, axis=-1)
    else:
        o_mem = omem_ref[...]

    z = _dot(xn, wz_ref[...])
    sz = z * _sigmoid(z)
    ya = _dot((ofox_ref[...] * sz[:, :FOX_WIDTH]).astype(BF16), wpa_ref[...])
    yb = _dot((ohg_ref[...] * sz[:, FOX_WIDTH:FOX_WIDTH + HG_WIDTH]).astype(BF16), wpb_ref[...])
    ym = _dot((o_mem * sz[:, FOX_WIDTH + HG_WIDTH:]).astype(BF16), wpm_ref[...])
    d = x.shape[1]
    gates = _sigmoid(_dot(xn, wg_ref[...]))
    mixed = gates[:, :d] * ya + gates[:, d:2 * d] * yb + gates[:, 2 * d:] * ym
    out = x + _dot(mixed.astype(BF16), wo_ref[...])
    y_ref[...] = _rms_rows(out, fg_ref[...]) if final else out


def _merge(x, g, o_fox, o_hg, w, final_g, *, final, mem_kv=None, o_mem=None):
    T, D = x.shape
    shared = mem_kv is not None
    tm = min(PROJ_TILE, T)
    assert T % tm == 0
    row = lambda width: pl.BlockSpec((tm, width), lambda i: (i, 0))
    full = lambda a: pl.BlockSpec(a.shape, lambda i: (0,) * a.ndim)
    tail = (w['z'], w['g'], w['pa'], w['pb'], w['pm'], w['o'], final_g)
    if shared:
        ins = (x, g, o_fox, o_hg, w['mq'], mem_kv[0], mem_kv[1]) + tail
        in_specs = [row(D), full(g), row(FOX_WIDTH), row(HG_WIDTH)] + [full(a) for a in ins[4:]]
    else:
        ins = (x, g, o_fox, o_hg, o_mem) + tail
        in_specs = ([row(D), full(g), row(FOX_WIDTH), row(HG_WIDTH), row(MEM_WIDTH)]
                    + [full(a) for a in ins[5:]])
    return pl.pallas_call(
        functools.partial(_merge_kernel, shared_mem=shared, final=final),
        grid=(T // tm,), in_specs=in_specs, out_specs=row(D),
        out_shape=jax.ShapeDtypeStruct((T, D), F32),
        compiler_params=_cparams("arbitrary"),
        name="merge_prompt" if shared else "merge_sample")(*ins)


def _fox_decode_kernel(*refs, G, page):
    k_refs = refs[1:1 + G]
    v_refs = refs[1 + G:1 + 2 * G]
    lf_refs = refs[1 + 2 * G:1 + 3 * G]
    (q_ref, kn_ref, vn_ref, cn_ref, o_ref, qb_ref, acc_ref, m_ref, l_ref, carry_ref) = refs[1 + 3 * G:]
    step = pl.program_id(1)
    hd = FOX_HEAD_DIM

    @pl.when(step == 0)
    def _():
        qb_ref[...] = jnp.broadcast_to(q_ref[...], qb_ref.shape)
        acc_ref[...] = jnp.zeros_like(acc_ref)
        m_ref[...] = jnp.full_like(m_ref, NEG_BIG)
        l_ref[...] = jnp.zeros_like(l_ref)
        carry_ref[...] = jnp.zeros_like(carry_ref)

    lane = lax.broadcasted_iota(jnp.int32, (FOX_HEADS, page), 1)
    later = carry_ref[...]
    cn = cn_ref[...]
    bias = []
    for g in range(G):
        lf = lf_refs[g][...]
        suf = lf
        sh = 1
        while sh < page:
            suf = suf + jnp.where(lane + sh < page, pltpu.roll(suf, page - sh, 1), 0.0)
            sh *= 2
        bias.append((suf - lf) + (later + cn))
        later = later + suf[:, 0:1]
    carry_ref[...] = later

    for h in range(FOX_HEADS):
        rows = slice(h * hd, (h + 1) * hd)
        qb = qb_ref[rows, :]
        s = [jnp.sum(k_refs[g][rows, :] * qb, axis=0, keepdims=True) + bias[g][h:h + 1, :] for g in range(G)]
        m_old = m_ref[h:h + 1, :]
        m_new = m_old
        for sg in s:
            m_new = jnp.maximum(m_new, jnp.max(sg, axis=1, keepdims=True))
        alpha = jnp.exp(m_old - m_new)
        p = [jnp.exp(sg - m_new) for sg in s]
        l_new = alpha * l_ref[h:h + 1, :]
        upd = alpha * acc_ref[rows, :]
        for g in range(G):
            l_new = l_new + jnp.sum(p[g], axis=1, keepdims=True)
            upd = upd + v_refs[g][rows, :] * p[g]
        l_ref[h:h + 1, :] = l_new
        m_ref[h:h + 1, :] = m_new
        acc_ref[rows, :] = upd

    @pl.when(step == pl.num_programs(1) - 1)
    def _():
        for h in range(FOX_HEADS):
            rows = slice(h * hd, (h + 1) * hd)
            s_new = jnp.sum(q_ref[rows, :] * kn_ref[rows, :], axis=0, keepdims=True)
            m_old = m_ref[h:h + 1, :]
            m_new = jnp.maximum(m_old, s_new)
            alpha = jnp.exp(m_old - m_new)
            p_new = jnp.exp(s_new - m_new)
            l = alpha * l_ref[h:h + 1, :] + p_new
            o = alpha * jnp.sum(acc_ref[rows, :], axis=1, keepdims=True) + p_new * vn_ref[rows, :]
            o_ref[rows, :] = o / l


def _fox_decode(page_table, cache_kT, cache_vT, cache_lfT, q_col, kn_col, vn_col, cn_col, layer):
    DB, n_pages = page_table.shape
    page = cache_kT.shape[-1]
    G = min(PAGES_PER_STEP, n_pages)
    assert n_pages % G == 0

    def page_map(g):
        def index_map(b, s, pt):
            return (layer, pt[b, n_pages - 1 - (s * G + g)], 0, 0)
        return index_map

    kv_specs = [pl.BlockSpec((None, None, FOX_WIDTH, page), page_map(g)) for g in range(G)]
    lf_specs = [pl.BlockSpec((None, None, FOX_HEADS, page), page_map(g)) for g in range(G)]
    per_b = lambda rows: pl.BlockSpec((None, rows, 1), lambda b, s, pt: (b, 0, 0))
    grid_spec = pltpu.PrefetchScalarGridSpec(
        num_scalar_prefetch=1, grid=(DB, n_pages // G),
        in_specs=kv_specs + kv_specs + lf_specs + [per_b(FOX_WIDTH)] * 3 + [per_b(FOX_HEADS)],
        out_specs=per_b(FOX_WIDTH),
        scratch_shapes=[pltpu.VMEM((FOX_WIDTH, page), F32), pltpu.VMEM((FOX_WIDTH, page), F32),
                        pltpu.VMEM((FOX_HEADS, 1), F32), pltpu.VMEM((FOX_HEADS, 1), F32),
                        pltpu.VMEM((FOX_HEADS, 1), F32)])
    return pl.pallas_call(
        functools.partial(_fox_decode_kernel, G=G, page=page),
        grid_spec=grid_spec, out_shape=jax.ShapeDtypeStruct((DB, FOX_WIDTH, 1), F32),
        compiler_params=_cparams("arbitrary", "arbitrary"), name="fox_decode")(
            page_table, *([cache_kT] * G), *([cache_vT] * G), *([cache_lfT] * G),
            q_col, kn_col, vn_col, cn_col)


def _sample_mix_kernel(st_ref, hq_ref, hf_ref, hi_ref, lbp_ref, g_ref, mq_ref, mk_ref, mv_ref,
                       st_out_ref, ohg_ref, omem_ref, *, layer, n_slots):
    for h in range(HG_HEADS):
        lb = _lower_bound([lbp_ref[j, h] for j in range(n_slots)], layer)
        hq = hq_ref[h]
        q = hq * _sigmoid(hq)
        f = lb + (1.0 - lb) * _sigmoid(hf_ref[h])
        s_new = f * st_ref[h] + (1.0 - f) * hi_ref[h]
        st_out_ref[h] = s_new
        o = jnp.sum(s_new * q, axis=0, keepdims=True)
        ohg_ref[h] = _rms_rows(o, g_ref[h])

    q = mq_ref[...] * (MEM_HEAD_DIM ** -0.5)
    q8 = jnp.concatenate([q, jnp.zeros_like(q)], axis=0).astype(BF16)
    s = _dot_nt(q8, mk_ref[...].astype(BF16))
    own = (lax.broadcasted_iota(jnp.int32, s.shape, 1) % MEM_HEADS
           == lax.broadcasted_iota(jnp.int32, s.shape, 0))
    s = jnp.where(own, s, NEG_BIG)
    p = jnp.exp(s - jnp.max(s, axis=-1, keepdims=True))
    o = _dot(p.astype(BF16), mv_ref[...].astype(BF16)) / jnp.sum(p, axis=-1, keepdims=True)
    omem_ref[...] = o[:MEM_HEADS, :]


def _sample_mix(state, hq_col, hf_col, hi_row, lbp_col, hg_g_row, mq, mk_flat, mv_flat, layer):
    DB = state.shape[0]
    n_slots = lbp_col.shape[0]
    per_b = lambda a: pl.BlockSpec((None,) + a.shape[1:], lambda b: (b,) + (0,) * (a.ndim - 1))
    full = lambda a: pl.BlockSpec(a.shape, lambda b: (0,) * a.ndim)
    ins = (state, hq_col, hf_col, hi_row, lbp_col, hg_g_row, mq, mk_flat, mv_flat)
    in_specs = [per_b(state), per_b(hq_col), per_b(hf_col), per_b(hi_row), full(lbp_col), full(hg_g_row),
                per_b(mq), per_b(mk_flat), per_b(mv_flat)]
    out_shape = (jax.ShapeDtypeStruct(state.shape, F32),
                 jax.ShapeDtypeStruct((DB, HG_HEADS, 1, HG_HEAD_DIM), F32),
                 jax.ShapeDtypeStruct((DB, MEM_HEADS, MEM_HEAD_DIM), F32))
    out_specs = tuple(pl.BlockSpec((None,) + s.shape[1:], lambda b, n=len(s.shape): (b,) + (0,) * (n - 1))
                      for s in out_shape)
    return pl.pallas_call(
        functools.partial(_sample_mix_kernel, layer=layer, n_slots=n_slots),
        grid=(DB,), in_specs=in_specs, out_specs=out_specs, out_shape=out_shape,
        compiler_params=_cparams("arbitrary"), name="sample_mix")(*ins)


def _layer_weights(w_in, b_fgate, w_mk, w_mv, w_pa, w_pb, w_pm, w_o):
    fw, hw, mw = FOX_WIDTH, HG_WIDTH, MEM_WIDTH
    d = w_in.shape[0]
    o = 0
    qkv = w_in[:, o:o + 3 * fw]; o += 3 * fw
    ff = w_in[:, o:o + FOX_HEADS]; o += FOX_HEADS
    fz = w_in[:, o:o + fw]; o += fw
    h3 = w_in[:, o:o + 3 * hw]; o += 3 * hw
    hz = w_in[:, o:o + hw]; o += hw
    mq = w_in[:, o:o + mw]; o += mw
    mz = w_in[:, o:o + mw]; o += mw
    gates = w_in[:, o:o + 3 * d]; o += 3 * d
    assert o == w_in.shape[1]
    pad_heads = AUG - FOX_HEAD_DIM
    wT = w_in.T[:3 * fw].astype(BF16)
    qaugT = jnp.pad(wT[:fw].reshape(FOX_HEADS, FOX_HEAD_DIM, d), ((0, 0), (0, pad_heads), (0, 0)))
    kaug = jnp.pad(qkv[:, fw:2 * fw].astype(BF16).reshape(d, FOX_HEADS, FOX_HEAD_DIM),
                   ((0, 0), (0, 0), (0, pad_heads)))
    return dict(
        qkv=qkv.astype(BF16), kT=wT[fw:2 * fw], vT=wT[2 * fw:],
        qaugT=qaugT.reshape(AUG_WIDTH, d), kaug=kaug.reshape(d, AUG_WIDTH),
        ff=jnp.pad(ff, ((0, 0), (0, LANES - FOX_HEADS))).astype(BF16),
        bf=jnp.pad(b_fgate.astype(F32), (0, LANES - FOX_HEADS)).reshape(1, LANES),
        h=h3.astype(BF16), mq=mq.astype(BF16),
        z=jnp.concatenate([fz, hz, mz], axis=1).astype(BF16), g=gates.astype(BF16),
        mk=w_mk.astype(BF16), mv=w_mv.astype(BF16),
        pa=w_pa.astype(BF16), pb=w_pb.astype(BF16), pm=w_pm.astype(BF16), o=w_o.astype(BF16))


def kernel(x_prompt, x_sample, mem_prompt, cache_fox_k, cache_fox_v, cache_fox_logf, page_table, state_hgrn, cache_mem_k, cache_mem_v, norm_g, w_in, b_fgate, hg_norm_g, lb_param, mem_norm_g, w_mk, w_mv, w_pa, w_pb, w_pm, w_o, final_norm_g):
    depth = w_in.shape[0]
    B, T, D = x_prompt.shape
    DB, Tn, _ = x_sample.shape
    assert B == 1 and Tn == 1
    n_phys, page = cache_fox_k.shape[1], cache_fox_k.shape[2]
    n_mem = cache_mem_k.shape[2]
    n_slots = lb_param.shape[0]

    xp = x_prompt.reshape(T, D)
    xs = x_sample.reshape(DB, D)
    mem = mem_prompt.reshape(n_mem, D)
    cache_kT = jnp.transpose(cache_fox_k, (0, 1, 3, 4, 2)).reshape(depth, n_phys, FOX_WIDTH, page)
    cache_vT = jnp.transpose(cache_fox_v, (0, 1, 3, 4, 2)).reshape(depth, n_phys, FOX_WIDTH, page)
    cache_lfT = jnp.transpose(cache_fox_logf, (0, 1, 3, 2))
    mk_cache = cache_mem_k.reshape(depth, DB, n_mem * MEM_HEADS, MEM_HEAD_DIM)
    mv_cache = cache_mem_v.reshape(depth, DB, n_mem * MEM_HEADS, MEM_HEAD_DIM)
    lbp_col = lb_param.reshape(n_slots, HG_HEADS, HG_HEAD_DIM, 1)
    final_g = final_norm_g.reshape(1, D)
    row = lambda a: a.reshape(1, -1)
    tok_major = lambda aT: jnp.transpose(aT.reshape(FOX_HEADS, FOX_HEAD_DIM, T), (2, 0, 1))

    outs = [[] for _ in range(10)]
    for l in range(depth):
        w = _layer_weights(w_in[l], b_fgate[l], w_mk[l], w_mv[l], w_pa[l], w_pb[l], w_pm[l], w_o[l])
        g = row(norm_g[l])
        last = l == depth - 1

        kT, vT, vTb, kaug, qaug, lfT, h3 = _inproj_prompt(xp, g, w)
        o_fox = _fox_prompt(qaug, kaug, vTb)
        o_hg, s_p = _hgrn_prompt(h3, lb_param, row(hg_norm_g[l]), l)
        mk, mv = _mem_kv(mem, row(mem_norm_g[l]), w['mk'], w['mv'])
        xp = _merge(xp, g, o_fox, o_hg, w, final_g, final=last,
                    mem_kv=(mk.astype(BF16), mv.astype(BF16)))

        qs, ks, vs, lfs, h3s, mqs = _inproj_sample(xs, g, w)
        o_fox_s = _fox_decode(page_table, cache_kT, cache_vT, cache_lfT, qs.reshape(DB, FOX_WIDTH, 1),
                              ks.reshape(DB, FOX_WIDTH, 1), vs.reshape(DB, FOX_WIDTH, 1),
                              lfs.reshape(DB, FOX_HEADS, 1), l)
        col = lambda a: a.reshape(DB, HG_HEADS, HG_HEAD_DIM, 1)
        s_s, o_hg_s, o_mem_s = _sample_mix(
            state_hgrn[l], col(h3s[:, :HG_WIDTH]), col(h3s[:, HG_WIDTH:2 * HG_WIDTH]),
            h3s[:, 2 * HG_WIDTH:].reshape(DB, HG_HEADS, 1, HG_HEAD_DIM), lbp_col,
            hg_norm_g[l].reshape(HG_HEADS, 1, HG_HEAD_DIM),
            mqs.reshape(DB, MEM_HEADS, MEM_HEAD_DIM), mk_cache[l], mv_cache[l], l)
        xs = _merge(xs, g, o_fox_s.reshape(DB, FOX_WIDTH), o_hg_s.reshape(DB, HG_WIDTH), w, final_g,
                    final=last, o_mem=o_mem_s.reshape(DB, MEM_WIDTH))

        new = (tok_major(kT).reshape(B, T, FOX_HEADS, FOX_HEAD_DIM),
               tok_major(vT).reshape(B, T, FOX_HEADS, FOX_HEAD_DIM),
               jnp.transpose(lfT).reshape(B, T, FOX_HEADS), s_p.reshape(B, HG_HEADS, HG_HEAD_DIM, HG_HEAD_DIM),
               mk.reshape(B, n_mem, MEM_HEADS, MEM_HEAD_DIM), mv.reshape(B, n_mem, MEM_HEADS, MEM_HEAD_DIM),
               ks.reshape(DB, Tn, FOX_HEADS, FOX_HEAD_DIM), vs.reshape(DB, Tn, FOX_HEADS, FOX_HEAD_DIM),
               lfs.reshape(DB, Tn, FOX_HEADS), s_s)
        for acc, a in zip(outs, new):
            acc.append(a)

    return (xp.reshape(B, T, D), xs.reshape(DB, Tn, D)) + tuple(jnp.stack(a) for a in outs)
```
